```python
import jax, jax.numpy as jnp
from jax import lax
import numpy as np

D_MODEL = 1024
BATCH = 16
SEQ = 4096
DEPTH = 1
DEC_BATCH = 4
DEC_SEQ = 8192
PAST_LEN = 128

D_MIX = D_MODEL
D_ATTN = D_MIX // 2
D_POOL = D_MIX - D_ATTN
HEAD_DIM = 64
N_HEADS = D_ATTN // HEAD_DIM
N_KV_HEADS = 2
GROUP = N_HEADS // N_KV_HEADS
D_Q = N_HEADS * HEAD_DIM
D_KV = N_KV_HEADS * HEAD_DIM
POOL_WINDOWS = (2, 4, 8, 16)
N_POOL_GROUPS = len(POOL_WINDOWS)
POOL_GROUP_DIM = D_POOL // N_POOL_GROUPS
D_IN = D_Q + 2 * D_KV + D_POOL
D_FF = 4 * D_MODEL
GRID_W = 64
ROPE_THETA = 10000.0
Q_BLOCK = 128
EPS = 1e-6
N_MOD = 6

kernel_name = "hymba_attn_pool_encoder"


def _rmsnorm(x, g):
    xf = x.astype(jnp.float32)
    y = xf * lax.rsqrt(jnp.mean(xf * xf, axis=-1, keepdims=True) + EPS) * g.astype(jnp.float32)
    return y.astype(x.dtype)


def _axial_rope_tables(seq_len):
    rows = seq_len // GRID_W
    row_idx = jnp.repeat(jnp.arange(rows, dtype=jnp.float32), GRID_W)
    col_idx = jnp.tile(jnp.arange(GRID_W, dtype=jnp.float32), rows)
    n_freq = HEAD_DIM // 4
    inv_freq = 1.0 / (ROPE_THETA ** (jnp.arange(n_freq, dtype=jnp.float32) / n_freq))
    ang = jnp.concatenate([row_idx[:, None] * inv_freq, col_idx[:, None] * inv_freq], axis=-1)
    return jnp.cos(ang), jnp.sin(ang)


def _apply_rope(x, cos, sin):
    xf = x.astype(jnp.float32)
    x1, x2 = xf[..., :HEAD_DIM // 2], xf[..., HEAD_DIM // 2:]
    c = cos[None, :, None, :]
    s = sin[None, :, None, :]
    return jnp.concatenate([x1 * c - x2 * s, x2 * c + x1 * s], axis=-1).astype(x.dtype)


def _attention(q, k, v):
    b, s = q.shape[0], q.shape[1]
    nblk = s // Q_BLOCK
    qb = q.reshape(b, nblk, Q_BLOCK, N_KV_HEADS, GROUP, HEAD_DIM).transpose(1, 0, 2, 3, 4, 5)
    scale = HEAD_DIM ** -0.5

    def block(qblk):
        sc = jnp.einsum('bqkgd,bskd->bkgqs', qblk, k).astype(jnp.float32) * scale
        p = jax.nn.softmax(sc, axis=-1).astype(v.dtype)
        return jnp.einsum('bkgqs,bskd->bqkgd', p, v)

    ob = lax.map(block, qb)
    return ob.transpose(1, 0, 2, 3, 4, 5).reshape(b, s, D_Q)


def _multiscale_pool(u, w_pool, pool_scale):
    b, s, _ = u.shape
    uf = u.reshape(b, s, N_POOL_GROUPS, POOL_GROUP_DIM).astype(jnp.float32)
    cs = jnp.concatenate([jnp.zeros((b, 1, N_POOL_GROUPS, POOL_GROUP_DIM), jnp.float32),
                          jnp.cumsum(uf, axis=1)], axis=1)
    t = jnp.arange(s)
    pooled = []
    for gi, w in enumerate(POOL_WINDOWS):
        lo = jnp.clip(t - w // 2, 0, s)
        hi = jnp.clip(t - w // 2 + w, 0, s)
        csg = cs[:, :, gi, :]
        cnt = (hi - lo).astype(jnp.float32)[None, :, None]
        pooled.append((csg[:, hi, :] - csg[:, lo, :]) / cnt)
    pooled = jnp.stack(pooled, axis=2)
    mixed = (pooled - uf).astype(u.dtype)
    out = jnp.einsum('bsgc,gcd->bsgd', mixed, w_pool).reshape(b, s, D_POOL)
    return out * pool_scale


def _layer(x, c, w_ada, b_ada, g_pre_mix, g_post_mix, g_pre_mlp, g_post_mlp,
           w_in, g_q, g_k, w_pool, pool_scale, w_out, w_ff1, w_ff2):
    b, s, _ = x.shape
    mod = (jax.nn.silu(c) @ w_ada + b_ada).reshape(b, N_MOD, 1, D_MODEL)
    shift_a, scale_a, gate_a = mod[:, 0], mod[:, 1], mod[:, 2]
    shift_m, scale_m, gate_m = mod[:, 3], mod[:, 4], mod[:, 5]

    h = _rmsnorm(x, g_pre_mix) * (1.0 + scale_a) + shift_a
    proj = h @ w_in
    q = proj[..., :D_Q].reshape(b, s, N_HEADS, HEAD_DIM)
    k = proj[..., D_Q:D_Q + D_KV].reshape(b, s, N_KV_HEADS, HEAD_DIM)
    v = proj[..., D_Q + D_KV:D_Q + 2 * D_KV].reshape(b, s, N_KV_HEADS, HEAD_DIM)
    u = proj[..., D_Q + 2 * D_KV:]
    cos, sin = _axial_rope_tables(s)
    q = _apply_rope(_rmsnorm(q, g_q), cos, sin)
    k = _apply_rope(_rmsnorm(k, g_k), cos, sin)
    a = _attention(q, k, v)
    p = _multiscale_pool(u, w_pool, pool_scale)
    mix = jnp.concatenate([a, p], axis=-1) @ w_out
    x = x + gate_a * _rmsnorm(mix, g_post_mix)

    h = _rmsnorm(x, g_pre_mlp) * (1.0 + scale_m) + shift_m
    f = jnp.square(jax.nn.relu(h @ w_ff1)) @ w_ff2
    x = x + gate_m * _rmsnorm(f, g_post_mlp)
    return x


def setup_inputs(seed: int = 0) -> dict:
    key = jax.random.key(seed)
    ks = jax.random.split(key, 20)
    f32 = jnp.float32

    def nrm(k, shape, scale):
        return jax.random.normal(k, shape, f32) * scale

    def gain(k, shape):
        return 1.0 + 0.02 * jax.random.normal(k, shape, f32)

    return {
        "x_prompt": nrm(ks[0], (BATCH, SEQ, D_MODEL), 1.0),
        "x_sample": nrm(ks[1], (DEC_BATCH, DEC_SEQ, D_MODEL), 1.0),
        "c_prompt": nrm(ks[2], (BATCH, D_MODEL), 1.0),
        "c_sample": nrm(ks[3], (DEC_BATCH, D_MODEL), 1.0),
        "w_ada": nrm(ks[4], (DEPTH, D_MODEL, N_MOD * D_MODEL), 0.5 * D_MODEL ** -0.5),
        "b_ada": nrm(ks[5], (DEPTH, N_MOD * D_MODEL), 0.01),
        "g_pre_mix": gain(ks[6], (DEPTH, D_MODEL)),
        "g_post_mix": gain(ks[7], (DEPTH, D_MODEL)),
        "g_pre_mlp": gain(ks[8], (DEPTH, D_MODEL)),
        "g_post_mlp": gain(ks[9], (DEPTH, D_MODEL)),
        "w_in": nrm(ks[10], (DEPTH, D_MODEL, D_IN), D_MODEL ** -0.5),
        "g_q": gain(ks[11], (DEPTH, HEAD_DIM)),
        "g_k": gain(ks[12], (DEPTH, HEAD_DIM)),
        "w_pool": nrm(ks[13], (DEPTH, N_POOL_GROUPS, POOL_GROUP_DIM, POOL_GROUP_DIM), POOL_GROUP_DIM ** -0.5),
        "pool_scale": gain(ks[14], (DEPTH, D_POOL)),
        "w_out": nrm(ks[15], (DEPTH, D_MIX, D_MODEL), D_MIX ** -0.5),
        "w_ff1": nrm(ks[16], (DEPTH, D_MODEL, D_FF), D_MODEL ** -0.5),
        "w_ff2": nrm(ks[17], (DEPTH, D_FF, D_MODEL), D_FF ** -0.5),
    }


def reference(x_prompt, x_sample, c_prompt, c_sample, w_ada, b_ada, g_pre_mix, g_post_mix,
              g_pre_mlp, g_post_mlp, w_in, g_q, g_k, w_pool, pool_scale, w_out, w_ff1, w_ff2):
    y_prompt = x_prompt
    y_sample = x_sample
    for l in range(DEPTH):
        y_prompt = _layer(y_prompt, c_prompt, w_ada[l], b_ada[l], g_pre_mix[l], g_post_mix[l],
                          g_pre_mlp[l], g_post_mlp[l], w_in[l], g_q[l], g_k[l], w_pool[l],
                          pool_scale[l], w_out[l], w_ff1[l], w_ff2[l])
        y_sample = _layer(y_sample, c_sample, w_ada[l], b_ada[l], g_pre_mix[l], g_post_mix[l],
                          g_pre_mlp[l], g_post_mlp[l], w_in[l], g_q[l], g_k[l], w_pool[l],
                          pool_scale[l], w_out[l], w_ff1[l], w_ff2[l])
    return (y_prompt, y_sample)
```

```python
import functools

import jax
import jax.numpy as jnp
from jax import lax
from jax.experimental import pallas as pl
from jax.experimental.pallas import tpu as pltpu

HEAD_DIM = 64
N_HEADS = 8
N_KV_HEADS = 2
GROUP = N_HEADS // N_KV_HEADS
D_Q = N_HEADS * HEAD_DIM
D_KV = N_KV_HEADS * HEAD_DIM
POOL_WINDOWS = (2, 4, 8, 16)
POOL_GROUP_DIM = 128
D_POOL = POOL_GROUP_DIM * len(POOL_WINDOWS)
GRID_W = 64
ROPE_THETA = 10000.0
EPS = 1e-6
N_MOD = 6
LANES = 128
HALO = 16
VMEM_LIMIT = 56 * 1024 * 1024

F32 = jnp.float32
BF16 = jnp.bfloat16


def _rms(x):
    return x * lax.rsqrt(jnp.mean(x * x, axis=-1, keepdims=True) + EPS)


def _mod_kernel(c_ref, w_ref, b_ref, o_ref):
    c = c_ref[...]
    sc = c / (1.0 + jnp.exp(-c))
    o_ref[...] = jnp.dot(sc, w_ref[...], precision=lax.Precision.HIGHEST,
                         preferred_element_type=F32) + b_ref[...]


def _mod_call(c, w_ada, b_ada):
    m, d = c.shape
    n = w_ada.shape[1]
    tn = 1024
    return pl.pallas_call(
        _mod_kernel,
        grid=(n // tn,),
        in_specs=[pl.BlockSpec((m, d), lambda j: (0, 0)),
                  pl.BlockSpec((d, tn), lambda j: (0, j)),
                  pl.BlockSpec((1, tn), lambda j: (0, j))],
        out_specs=pl.BlockSpec((m, tn), lambda j: (0, j)),
        out_shape=jax.ShapeDtypeStruct((m, n), F32),
        compiler_params=pltpu.CompilerParams(dimension_semantics=("arbitrary",),
                                             vmem_limit_bytes=VMEM_LIMIT),
        name="mod",
    )(c, w_ada, b_ada.reshape(1, n))


def _head_norm_rope(p, gain, cos_t, sin_t):
    lane = lax.broadcasted_iota(jnp.int32, p.shape, 1)
    lo = lane < HEAD_DIM
    sq = p * p
    s_lo = jnp.sum(jnp.where(lo, sq, 0.0), axis=-1, keepdims=True)
    s_hi = jnp.sum(jnp.where(lo, 0.0, sq), axis=-1, keepdims=True)
    ms = jnp.where(lo, s_lo, s_hi) * (1.0 / HEAD_DIM)
    pn = p * lax.rsqrt(ms + EPS) * gain
    fwd = pltpu.roll(pn, LANES - HEAD_DIM // 2, 1)
    bwd = pltpu.roll(pn, HEAD_DIM // 2, 1)
    rot = jnp.where((lane % HEAD_DIM) < HEAD_DIM // 2, fwd, bwd)
    return pn * cos_t + rot * sin_t


def _in_kernel(x_ref, shift_ref, scale_ref, g_ref, w_ref, gq_ref, gk_ref, cos_ref, sin_ref,
               q_ref, kt_ref, v_ref, u_ref):
    x = x_ref[0]
    mult = g_ref[...] * (1.0 + scale_ref[0])
    h = _rms(x) * mult + shift_ref[0]
    proj = jnp.dot(h.astype(BF16), w_ref[...], preferred_element_type=F32)
    cos_t = cos_ref[...]
    sin_t = sin_ref[...]
    for j in range(D_Q // LANES):
        qj = _head_norm_rope(proj[:, j * LANES:(j + 1) * LANES], gq_ref[...], cos_t, sin_t)
        q_ref[0, :, j * LANES:(j + 1) * LANES] = qj.astype(BF16)
    k = _head_norm_rope(proj[:, D_Q:D_Q + D_KV], gk_ref[...], cos_t, sin_t)
    kt_ref[0] = k.T.astype(BF16)
    v = proj[:, D_Q + D_KV:D_Q + 2 * D_KV].astype(BF16)
    for kv in range(N_KV_HEADS):
        v_ref[0, kv] = v[:, kv * HEAD_DIM:(kv + 1) * HEAD_DIM]
    u_ref[0] = proj[:, D_Q + 2 * D_KV:].astype(BF16)


def _in_call(x, shift, scale, g_pre, w_in, gq2, gk2, cos_t, sin_t, tm):
    b, s, d = x.shape
    d_in = w_in.shape[1]
    row = lambda i, j: (i, 0, 0)
    const2 = lambda i, j: (0, 0)
    return pl.pallas_call(
        _in_kernel,
        grid=(b, s // tm),
        in_specs=[pl.BlockSpec((1, tm, d), lambda i, j: (i, j, 0)),
                  pl.BlockSpec((1, 1, d), row),
                  pl.BlockSpec((1, 1, d), row),
                  pl.BlockSpec((1, d), const2),
                  pl.BlockSpec((d, d_in), const2),
                  pl.BlockSpec((1, LANES), const2),
                  pl.BlockSpec((1, LANES), const2),
                  pl.BlockSpec((tm, LANES), lambda i, j: (j, 0)),
                  pl.BlockSpec((tm, LANES), lambda i, j: (j, 0))],
        out_specs=[pl.BlockSpec((1, tm, D_Q), lambda i, j: (i, j, 0)),
                   pl.BlockSpec((1, D_KV, tm), lambda i, j: (i, 0, j)),
                   pl.BlockSpec((1, N_KV_HEADS, tm, HEAD_DIM), lambda i, j: (i, 0, j, 0)),
                   pl.BlockSpec((1, tm, D_POOL), lambda i, j: (i, j, 0))],
        out_shape=[jax.ShapeDtypeStruct((b, s, D_Q), BF16),
                   jax.ShapeDtypeStruct((b, D_KV, s), BF16),
                   jax.ShapeDtypeStruct((b, N_KV_HEADS, s, HEAD_DIM), BF16),
                   jax.ShapeDtypeStruct((b, s, D_POOL), BF16)],
        compiler_params=pltpu.CompilerParams(dimension_semantics=("arbitrary", "arbitrary"),
                                             vmem_limit_bytes=VMEM_LIMIT),
        name="in_proj",
    )(x, shift, scale, g_pre, w_in, gq2, gk2, cos_t, sin_t)


def _attn_kernel(q_ref, kt_ref, v_ref, o_ref, m_scr, l_scr, acc_scr, *, tq, tk, nk):
    q = q_ref[0]
    q4 = jnp.concatenate([q[:, g * HEAD_DIM:(g + 1) * HEAD_DIM] for g in range(GROUP)], axis=0)
    m_scr[...] = jnp.full(m_scr.shape, -jnp.inf, F32)
    l_scr[...] = jnp.zeros(l_scr.shape, F32)
    acc_scr[...] = jnp.zeros(acc_scr.shape, F32)

    def body(kc, carry):
        off = pl.multiple_of(kc * tk, tk)
        kt = kt_ref[0, :, pl.ds(off, tk)]
        s = jnp.dot(q4, kt, preferred_element_type=F32)
        m_prev = m_scr[...]
        m_new = jnp.maximum(m_prev, jnp.max(s, axis=1, keepdims=True))
        alpha = jnp.exp(m_prev - m_new)
        p = jnp.exp(s - m_new)
        l_scr[...] = alpha * l_scr[...] + jnp.sum(p, axis=1, keepdims=True)
        v = v_ref[0, 0, pl.ds(off, tk), :]
        acc_scr[...] = alpha * acc_scr[...] + jnp.dot(p.astype(BF16), v, preferred_element_type=F32)
        m_scr[...] = m_new
        return carry

    lax.fori_loop(0, nk, body, 0)
    o = acc_scr[...] / l_scr[...]
    o_ref[0] = jnp.concatenate([o[g * tq:(g + 1) * tq] for g in range(GROUP)], axis=1).astype(BF16)


def _attn_call(q, kt, v, tq, tk):
    b, s, _ = q.shape
    kern = functools.partial(_attn_kernel, tq=tq, tk=tk, nk=s // tk)
    gw = GROUP * HEAD_DIM
    return pl.pallas_call(
        kern,
        grid=(b, N_KV_HEADS, s // tq),
        in_specs=[pl.BlockSpec((1, tq, gw), lambda i, h, j: (i, j, h)),
                  pl.BlockSpec((1, HEAD_DIM, s), lambda i, h, j: (i, h, 0)),
                  pl.BlockSpec((1, 1, s, HEAD_DIM), lambda i, h, j: (i, h, 0, 0))],
        out_specs=pl.BlockSpec((1, tq, gw), lambda i, h, j: (i, j, h)),
        out_shape=jax.ShapeDtypeStruct((b, s, D_Q), BF16),
        scratch_shapes=[pltpu.VMEM((GROUP * tq, 1), F32),
                        pltpu.VMEM((GROUP * tq, 1), F32),
                        pltpu.VMEM((GROUP * tq, HEAD_DIM), F32)],
        compiler_params=pltpu.CompilerParams(
            dimension_semantics=("arbitrary", "arbitrary", "arbitrary"),
            vmem_limit_bytes=VMEM_LIMIT),
        name="attention",
    )(q, kt, v)


def _post_kernel(x_ref, a_ref, u_ref, up_ref, un_ref, mod_ref, gains_ref, ps_ref,
                 wp_ref, wo_ref, w1_ref, w2_ref, y_ref, ext_scr, *, tm, seq, ff_chunk):
    j = pl.program_id(1)
    nj = pl.num_programs(1)
    u_main = u_ref[0].astype(F32)
    ext_scr[0:HALO, :] = jnp.where(j > 0, up_ref[0].astype(F32), 0.0)
    ext_scr[HALO:HALO + tm, :] = u_main
    ext_scr[HALO + tm:, :] = jnp.where(j < nj - 1, un_ref[0].astype(F32), 0.0)
    t = j * tm + lax.broadcasted_iota(jnp.int32, (tm, 1), 0)
    pooled = []
    for gi, w in enumerate(POOL_WINDOWS):
        cols = slice(gi * POOL_GROUP_DIM, (gi + 1) * POOL_GROUP_DIM)
        acc = ext_scr[HALO - w // 2:HALO - w // 2 + tm, cols]
        for dlt in range(1, w):
            start = HALO - w // 2 + dlt
            acc = acc + ext_scr[start:start + tm, cols]
        lo = jnp.clip(t - w // 2, 0, seq)
        hi = jnp.clip(t - w // 2 + w, 0, seq)
        cnt = (hi - lo).astype(F32)
        mixed = (acc / cnt - u_main[:, cols]).astype(BF16)
        pg = jnp.dot(mixed, wp_ref[gi], preferred_element_type=F32)
        pooled.append(pg * ps_ref[:, cols])
    p = jnp.concatenate(pooled, axis=1).astype(BF16)

    mix = (jnp.dot(a_ref[0], wo_ref[0:D_Q, :], preferred_element_type=F32)
           + jnp.dot(p, wo_ref[D_Q:, :], preferred_element_type=F32))
    gate_a = mod_ref[0, 2:3, :]
    shift_m = mod_ref[0, 3:4, :]
    scale_m = mod_ref[0, 4:5, :]
    gate_m = mod_ref[0, 5:6, :]
    x1 = x_ref[0] + gate_a * (_rms(mix) * gains_ref[0:1, :])
    h = (_rms(x1) * (gains_ref[1:2, :] * (1.0 + scale_m)) + shift_m).astype(BF16)
    d_ff = w1_ref.shape[1]
    f = jnp.zeros(x1.shape, F32)
    for c in range(d_ff // ff_chunk):
        hc = jnp.dot(h, w1_ref[:, c * ff_chunk:(c + 1) * ff_chunk], preferred_element_type=F32)
        hc = jnp.square(jnp.maximum(hc, 0.0)).astype(BF16)
        f = f + jnp.dot(hc, w2_ref[c * ff_chunk:(c + 1) * ff_chunk, :], preferred_element_type=F32)
    y_ref[0] = x1 + gate_m * (_rms(f) * gains_ref[2:3, :])


def _post_call(x, a, u, mod, gains, pool_scale, w_pool, w_out, w_ff1, w_ff2, tm, ff_chunk):
    b, s, d = x.shape
    d_ff = w_ff1.shape[1]
    nh = tm // HALO
    n_halo = s // HALO
    kern = functools.partial(_post_kernel, tm=tm, seq=s, ff_chunk=ff_chunk)
    const2 = lambda i, j: (0, 0)
    const3 = lambda i, j: (0, 0, 0)
    return pl.pallas_call(
        kern,
        grid=(b, s // tm),
        in_specs=[pl.BlockSpec((1, tm, d), lambda i, j: (i, j, 0)),
                  pl.BlockSpec((1, tm, D_Q), lambda i, j: (i, j, 0)),
                  pl.BlockSpec((1, tm, D_POOL), lambda i, j: (i, j, 0)),
                  pl.BlockSpec((1, HALO, D_POOL), lambda i, j: (i, jnp.maximum(j * nh - 1, 0), 0)),
                  pl.BlockSpec((1, HALO, D_POOL),
                               lambda i, j: (i, jnp.minimum((j + 1) * nh, n_halo - 1), 0)),
                  pl.BlockSpec((1, N_MOD, d), lambda i, j: (i, 0, 0)),
                  pl.BlockSpec((3, d), const2),
                  pl.BlockSpec((1, D_POOL), const2),
                  pl.BlockSpec(w_pool.shape, const3),
                  pl.BlockSpec(w_out.shape, const2),
                  pl.BlockSpec((d, d_ff), const2),
                  pl.BlockSpec((d_ff, d), const2)],
        out_specs=pl.BlockSpec((1, tm, d), lambda i, j: (i, j, 0)),
        out_shape=jax.ShapeDtypeStruct((b, s, d), F32),
        scratch_shapes=[pltpu.VMEM((tm + 2 * HALO, D_POOL), F32)],
        compiler_params=pltpu.CompilerParams(dimension_semantics=("arbitrary", "arbitrary"),
                                             vmem_limit_bytes=VMEM_LIMIT),
        name="post_mlp",
    )(x, a, u, u, u, mod, gains, pool_scale, w_pool, w_out, w_ff1, w_ff2)


def _rope_tables(seq_len):
    rows = seq_len // GRID_W
    row_idx = jnp.repeat(jnp.arange(rows, dtype=F32), GRID_W)
    col_idx = jnp.tile(jnp.arange(GRID_W, dtype=F32), rows)
    n_freq = HEAD_DIM // 4
    inv_freq = 1.0 / (ROPE_THETA ** (jnp.arange(n_freq, dtype=F32) / n_freq))
    ang = jnp.concatenate([row_idx[:, None] * inv_freq, col_idx[:, None] * inv_freq], axis=-1)
    cos, sin = jnp.cos(ang), jnp.sin(ang)
    cos_t = jnp.tile(cos, (1, LANES // (HEAD_DIM // 2)))
    sin_t = jnp.tile(jnp.concatenate([-sin, sin], axis=-1), (1, LANES // HEAD_DIM))
    return cos_t, sin_t


def _tiles(s):
    tm_in = min(512, s)
    tm_post = min(256, s)
    tq = min(128, s)
    tk = min(1024, s)
    return tm_in, tm_post, tq, tk


def _layer(x, mod, g_pre_mix, g_post_mix, g_pre_mlp, g_post_mlp, w_in, g_q, g_k, w_pool,
           pool_scale, w_out, w_ff1, w_ff2):
    b, s, d = x.shape
    tm_in, tm_post, tq, tk = _tiles(s)
    mod = mod.reshape(b, N_MOD, d)
    cos_t, sin_t = _rope_tables(s)
    gq2 = (jnp.tile(g_q, LANES // HEAD_DIM) * (HEAD_DIM ** -0.5)).reshape(1, LANES)
    gk2 = jnp.tile(g_k, LANES // HEAD_DIM).reshape(1, LANES)
    q, kt, v, u = _in_call(x, mod[:, 0:1], mod[:, 1:2], g_pre_mix.reshape(1, d), w_in, gq2, gk2,
                           cos_t, sin_t, tm_in)
    a = _attn_call(q, kt, v, tq, tk)
    gains = jnp.stack([g_post_mix, g_pre_mlp, g_post_mlp])
    return _post_call(x, a, u, mod, gains, pool_scale.reshape(1, D_POOL), w_pool, w_out, w_ff1,
                      w_ff2, tm_post, 1024)


def kernel(x_prompt, x_sample, c_prompt, c_sample, w_ada, b_ada, g_pre_mix, g_post_mix,
           g_pre_mlp, g_post_mlp, w_in, g_q, g_k, w_pool, pool_scale, w_out, w_ff1, w_ff2):
    depth = w_ada.shape[0]
    bp, bs = c_prompt.shape[0], c_sample.shape[0]
    pad = (-(bp + bs)) % 8
    y_prompt, y_sample = x_prompt, x_sample
    for l in range(depth):
        c_all = jnp.concatenate([c_prompt, c_sample, jnp.zeros((pad, c_prompt.shape[1]), F32)])
        mod = _mod_call(c_all, w_ada[l], b_ada[l])
        weights = (g_pre_mix[l], g_post_mix[l], g_pre_mlp[l], g_post_mlp[l],
                   w_in[l].astype(BF16), g_q[l], g_k[l], w_pool[l].astype(BF16), pool_scale[l],
                   w_out[l].astype(BF16), w_ff1[l].astype(BF16), w_ff2[l].astype(BF16))
        y_prompt = _layer(y_prompt, mod[:bp], *weights)
        y_sample = _layer(y_sample, mod[bp:bp + bs], *weights)
    return (y_prompt, y_sample)
```

```python
import functools

import jax
import jax.numpy as jnp
from jax import lax
from jax.experimental import pallas as pl
from jax.experimental.pallas import tpu as pltpu

HEAD_DIM = 64
N_HEADS = 8
N_KV_HEADS = 2
GROUP = N_HEADS // N_KV_HEADS
D_Q = N_HEADS * HEAD_DIM
D_KV = N_KV_HEADS * HEAD_DIM
POOL_WINDOWS = (2, 4, 8, 16)
POOL_GROUP_DIM = 128
D_POOL = POOL_GROUP_DIM * len(POOL_WINDOWS)
GRID_W = 64
ROPE_THETA = 10000.0
EPS = 1e-6
N_MOD = 6
LOG2_E = 1.4426950408889634
LANES = 128
HALO = 16
VMEM_LIMIT = 56 * 1024 * 1024

F32 = jnp.float32
BF16 = jnp.bfloat16


def _rms(x):
    return x * lax.rsqrt(jnp.mean(x * x, axis=-1, keepdims=True) + EPS)


def _mod_kernel(c_ref, w_ref, b_ref, o_ref):
    c = c_ref[...]
    sc = c / (1.0 + jnp.exp(-c))
    o_ref[...] = jnp.dot(sc, w_ref[...], precision=lax.Precision.HIGHEST,
                         preferred_element_type=F32) + b_ref[...]


def _mod_call(c, w_ada, b_ada):
    m, d = c.shape
    n = w_ada.shape[1]
    tn = 1024
    return pl.pallas_call(
        _mod_kernel,
        grid=(n // tn,),
        in_specs=[pl.BlockSpec((m, d), lambda j: (0, 0)),
                  pl.BlockSpec((d, tn), lambda j: (0, j)),
                  pl.BlockSpec((1, tn), lambda j: (0, j))],
        out_specs=pl.BlockSpec((m, tn), lambda j: (0, j)),
        out_shape=jax.ShapeDtypeStruct((m, n), F32),
        compiler_params=pltpu.CompilerParams(dimension_semantics=("arbitrary",),
                                             vmem_limit_bytes=VMEM_LIMIT),
        name="mod",
    )(c, w_ada, b_ada.reshape(1, n))


def _head_norm_rope(p, gain, cos_t, sin_t):
    lane = lax.broadcasted_iota(jnp.int32, p.shape, 1)
    lo = lane < HEAD_DIM
    sq = p * p
    s_lo = jnp.sum(jnp.where(lo, sq, 0.0), axis=-1, keepdims=True)
    s_hi = jnp.sum(jnp.where(lo, 0.0, sq), axis=-1, keepdims=True)
    ms = jnp.where(lo, s_lo, s_hi) * (1.0 / HEAD_DIM)
    pn = p * lax.rsqrt(ms + EPS) * gain
    fwd = pltpu.roll(pn, LANES - HEAD_DIM // 2, 1)
    bwd = pltpu.roll(pn, HEAD_DIM // 2, 1)
    rot = jnp.where((lane % HEAD_DIM) < HEAD_DIM // 2, fwd, bwd)
    return pn * cos_t + rot * sin_t


def _in_kernel(x_ref, shift_ref, scale_ref, g_ref, w_ref, gq_ref, gk_ref, cos_ref, sin_ref,
               qt_ref, k_ref, vt_ref, u_ref):
    x = x_ref[0]
    mult = g_ref[...] * (1.0 + scale_ref[0])
    h = _rms(x) * mult + shift_ref[0]
    proj = jnp.dot(h.astype(BF16), w_ref[...], preferred_element_type=F32)
    cos_t = cos_ref[...]
    sin_t = sin_ref[...]
    for j in range(D_Q // LANES):
        qj = _head_norm_rope(proj[:, j * LANES:(j + 1) * LANES], gq_ref[...], cos_t, sin_t)
        qt_ref[0, j * LANES:(j + 1) * LANES, :] = qj.T.astype(BF16)
    k_ref[0] = _head_norm_rope(proj[:, D_Q:D_Q + D_KV], gk_ref[...], cos_t, sin_t).astype(BF16)
    vt_ref[0] = proj[:, D_Q + D_KV:D_Q + 2 * D_KV].T.astype(BF16)
    u_ref[0] = proj[:, D_Q + 2 * D_KV:].astype(BF16)


def _in_call(x, shift, scale, g_pre, w_in, gq2, gk2, cos_t, sin_t, tm):
    b, s, d = x.shape
    d_in = w_in.shape[1]
    row = lambda i, j: (i, 0, 0)
    const2 = lambda i, j: (0, 0)
    return pl.pallas_call(
        _in_kernel,
        grid=(b, s // tm),
        in_specs=[pl.BlockSpec((1, tm, d), lambda i, j: (i, j, 0)),
                  pl.BlockSpec((1, 1, d), row),
                  pl.BlockSpec((1, 1, d), row),
                  pl.BlockSpec((1, d), const2),
                  pl.BlockSpec((d, d_in), const2),
                  pl.BlockSpec((1, LANES), const2),
                  pl.BlockSpec((1, LANES), const2),
                  pl.BlockSpec((tm, LANES), lambda i, j: (j, 0)),
                  pl.BlockSpec((tm, LANES), lambda i, j: (j, 0))],
        out_specs=[pl.BlockSpec((1, D_Q, tm), lambda i, j: (i, 0, j)),
                   pl.BlockSpec((1, tm, D_KV), lambda i, j: (i, j, 0)),
                   pl.BlockSpec((1, D_KV, tm), lambda i, j: (i, 0, j)),
                   pl.BlockSpec((1, tm, D_POOL), lambda i, j: (i, j, 0))],
        out_shape=[jax.ShapeDtypeStruct((b, D_Q, s), BF16),
                   jax.ShapeDtypeStruct((b, s, D_KV), BF16),
                   jax.ShapeDtypeStruct((b, D_KV, s), BF16),
                   jax.ShapeDtypeStruct((b, s, D_POOL), BF16)],
        compiler_params=pltpu.CompilerParams(dimension_semantics=("arbitrary", "arbitrary"),
                                             vmem_limit_bytes=VMEM_LIMIT),
        name="in_proj",
    )(x, shift, scale, g_pre, w_in, gq2, gk2, cos_t, sin_t)


def _attn_kernel(qt_ref, k_ref, vt_ref, o_ref, s_scr, p_scr, acc_scr, *, tq, tk, nk):
    mq = GROUP * tq
    qt = qt_ref[0]
    q_t = jnp.concatenate([qt[g * HEAD_DIM:(g + 1) * HEAD_DIM, :] for g in range(GROUP)], axis=1)
    zeros = jnp.zeros_like(q_t)
    q_ext = jnp.where(pl.program_id(1) == 0, jnp.concatenate([q_t, zeros], axis=0),
                      jnp.concatenate([zeros, q_t], axis=0))
    acc_scr[...] = jnp.zeros(acc_scr.shape, F32)

    def scores(c, slot):
        off = pl.multiple_of(c * tk, tk)
        s_t = jnp.dot(k_ref[0, pl.ds(off, tk), :], q_ext, preferred_element_type=F32)
        s_scr[slot] = s_t
        return jnp.max(s_t, axis=0, keepdims=True)

    def numerator(slot, cmax, m_prev, l_prev):
        m_new = jnp.maximum(m_prev, cmax)
        alpha = jnp.exp2(m_prev - m_new)
        p = jnp.exp2(s_scr[slot] - m_new)
        p_scr[slot] = p.astype(BF16)
        return m_new, alpha * l_prev + jnp.sum(p, axis=0, keepdims=True), alpha

    def values(c, slot, alpha):
        off = pl.multiple_of(c * tk, tk)
        pv = jnp.dot(vt_ref[0, :, pl.ds(off, tk)], p_scr[slot], preferred_element_type=F32)
        acc_scr[...] = alpha * acc_scr[...] + pv

    m = jnp.full((1, mq), -jnp.inf, F32)
    l = jnp.zeros((1, mq), F32)
    cmax = scores(0, 0)
    cmax_next = scores(1, 1)
    m, l, alpha = numerator(0, cmax, m, l)
    cmax = cmax_next

    def body(t, carry):
        cmax, m, l, alpha = carry
        c = 2 * t + 2
        for par in range(2):
            cmax_next = scores(c + par, par)
            m, l, alpha_next = numerator(1 - par, cmax, m, l)
            values(c + par - 2, par, alpha)
            cmax, alpha = cmax_next, alpha_next
        return cmax, m, l, alpha

    cmax, m, l, alpha = lax.fori_loop(0, (nk - 2) // 2, body, (cmax, m, l, alpha))
    m, l, alpha_last = numerator(1, cmax, m, l)
    values(nk - 2, 0, alpha)
    values(nk - 1, 1, alpha_last)
    o_t = acc_scr[...] / l
    o_ref[0] = jnp.concatenate([o_t[:, g * tq:(g + 1) * tq].T for g in range(GROUP)],
                               axis=1).astype(BF16)


def _attn_call(qt, k, vt, tq, tk):
    b, _, s = qt.shape
    nk = s // tk
    assert nk >= 2 and nk % 2 == 0
    kern = functools.partial(_attn_kernel, tq=tq, tk=tk, nk=nk)
    gw = GROUP * HEAD_DIM
    mq = GROUP * tq
    return pl.pallas_call(
        kern,
        grid=(b, N_KV_HEADS, s // tq),
        in_specs=[pl.BlockSpec((1, gw, tq), lambda i, h, j: (i, h, j)),
                  pl.BlockSpec((1, s, D_KV), lambda i, h, j: (i, 0, 0)),
                  pl.BlockSpec((1, HEAD_DIM, s), lambda i, h, j: (i, h, 0))],
        out_specs=pl.BlockSpec((1, tq, gw), lambda i, h, j: (i, j, h)),
        out_shape=jax.ShapeDtypeStruct((b, s, D_Q), BF16),
        scratch_shapes=[pltpu.VMEM((2, tk, mq), F32),
                        pltpu.VMEM((2, tk, mq), BF16),
                        pltpu.VMEM((HEAD_DIM, mq), F32)],
        compiler_params=pltpu.CompilerParams(
            dimension_semantics=("arbitrary", "arbitrary", "arbitrary"),
            vmem_limit_bytes=VMEM_LIMIT),
        name="attention",
    )(qt, k, vt)


def _post_kernel(x_ref, a_ref, u_ref, up_ref, un_ref, mod_ref, gains_ref, ps_ref,
                 wp_ref, wo_ref, w1_ref, w2_ref, y_ref, ext_scr, *, tm, seq, ff_chunk):
    j = pl.program_id(1)
    nj = pl.num_programs(1)
    u_main = u_ref[0].astype(F32)
    ext_scr[0:HALO, :] = jnp.where(j > 0, up_ref[0].astype(F32), 0.0)
    ext_scr[HALO:HALO + tm, :] = u_main
    ext_scr[HALO + tm:, :] = jnp.where(j < nj - 1, un_ref[0].astype(F32), 0.0)
    t = j * tm + lax.broadcasted_iota(jnp.int32, (tm, 1), 0)
    pooled = []
    for gi, w in enumerate(POOL_WINDOWS):
        cols = slice(gi * POOL_GROUP_DIM, (gi + 1) * POOL_GROUP_DIM)
        acc = ext_scr[HALO - w // 2:HALO - w // 2 + tm, cols]
        for dlt in range(1, w):
            start = HALO - w // 2 + dlt
            acc = acc + ext_scr[start:start + tm, cols]
        lo = jnp.clip(t - w // 2, 0, seq)
        hi = jnp.clip(t - w // 2 + w, 0, seq)
        cnt = (hi - lo).astype(F32)
        mixed = (acc / cnt - u_main[:, cols]).astype(BF16)
        pg = jnp.dot(mixed, wp_ref[gi], preferred_element_type=F32)
        pooled.append(pg * ps_ref[:, cols])
    p = jnp.concatenate(pooled, axis=1).astype(BF16)

    mix = (jnp.dot(a_ref[0], wo_ref[0:D_Q, :], preferred_element_type=F32)
           + jnp.dot(p, wo_ref[D_Q:, :], preferred_element_type=F32))
    gate_a = mod_ref[0, 2:3, :]
    shift_m = mod_ref[0, 3:4, :]
    scale_m = mod_ref[0, 4:5, :]
    gate_m = mod_ref[0, 5:6, :]
    x1 = x_ref[0] + gate_a * (_rms(mix) * gains_ref[0:1, :])
    h = (_rms(x1) * (gains_ref[1:2, :] * (1.0 + scale_m)) + shift_m).astype(BF16)
    d_ff = w1_ref.shape[1]
    f = jnp.zeros(x1.shape, F32)
    for c in range(d_ff // ff_chunk):
        hc = jnp.dot(h, w1_ref[:, c * ff_chunk:(c + 1) * ff_chunk], preferred_element_type=F32)
        hc = jnp.square(jnp.maximum(hc, 0.0)).astype(BF16)
        f = f + jnp.dot(hc, w2_ref[c * ff_chunk:(c + 1) * ff_chunk, :], preferred_element_type=F32)
    y_ref[0] = x1 + gate_m * (_rms(f) * gains_ref[2:3, :])


def _post_call(x, a, u, mod, gains, pool_scale, w_pool, w_out, w_ff1, w_ff2, tm, ff_chunk):
    b, s, d = x.shape
    d_ff = w_ff1.shape[1]
    nh = tm // HALO
    n_halo = s // HALO
    kern = functools.partial(_post_kernel, tm=tm, seq=s, ff_chunk=ff_chunk)
    const2 = lambda i, j: (0, 0)
    const3 = lambda i, j: (0, 0, 0)
    return pl.pallas_call(
        kern,
        grid=(b, s // tm),
        in_specs=[pl.BlockSpec((1, tm, d), lambda i, j: (i, j, 0)),
                  pl.BlockSpec((1, tm, D_Q), lambda i, j: (i, j, 0)),
                  pl.BlockSpec((1, tm, D_POOL), lambda i, j: (i, j, 0)),
                  pl.BlockSpec((1, HALO, D_POOL), lambda i, j: (i, jnp.maximum(j * nh - 1, 0), 0)),
                  pl.BlockSpec((1, HALO, D_POOL),
                               lambda i, j: (i, jnp.minimum((j + 1) * nh, n_halo - 1), 0)),
                  pl.BlockSpec((1, N_MOD, d), lambda i, j: (i, 0, 0)),
                  pl.BlockSpec((3, d), const2),
                  pl.BlockSpec((1, D_POOL), const2),
                  pl.BlockSpec(w_pool.shape, const3),
                  pl.BlockSpec(w_out.shape, const2),
                  pl.BlockSpec((d, d_ff), const2),
                  pl.BlockSpec((d_ff, d), const2)],
        out_specs=pl.BlockSpec((1, tm, d), lambda i, j: (i, j, 0)),
        out_shape=jax.ShapeDtypeStruct((b, s, d), F32),
        scratch_shapes=[pltpu.VMEM((tm + 2 * HALO, D_POOL), F32)],
        compiler_params=pltpu.CompilerParams(dimension_semantics=("arbitrary", "arbitrary"),
                                             vmem_limit_bytes=VMEM_LIMIT),
        name="post_mlp",
    )(x, a, u, u, u, mod, gains, pool_scale, w_pool, w_out, w_ff1, w_ff2)


def _rope_tables(seq_len):
    rows = seq_len // GRID_W
    row_idx = jnp.repeat(jnp.arange(rows, dtype=F32), GRID_W)
    col_idx = jnp.tile(jnp.arange(GRID_W, dtype=F32), rows)
    n_freq = HEAD_DIM // 4
    inv_freq = 1.0 / (ROPE_THETA ** (jnp.arange(n_freq, dtype=F32) / n_freq))
    ang = jnp.concatenate([row_idx[:, None] * inv_freq, col_idx[:, None] * inv_freq], axis=-1)
    cos, sin = jnp.cos(ang), jnp.sin(ang)
    cos_t = jnp.tile(cos, (1, LANES // (HEAD_DIM // 2)))
    sin_t = jnp.tile(jnp.concatenate([-sin, sin], axis=-1), (1, LANES // HEAD_DIM))
    return cos_t, sin_t


def _tiles(s):
    tm_in = min(512, s)
    tm_post = min(256, s)
    tq = min(256, s)
    tk = min(1024, s // 2)
    return tm_in, tm_post, tq, tk


def _layer(x, mod, g_pre_mix, g_post_mix, g_pre_mlp, g_post_mlp, w_in, g_q, g_k, w_pool,
           pool_scale, w_out, w_ff1, w_ff2):
    b, s, d = x.shape
    tm_in, tm_post, tq, tk = _tiles(s)
    mod = mod.reshape(b, N_MOD, d)
    cos_t, sin_t = _rope_tables(s)
    gq2 = (jnp.tile(g_q, LANES // HEAD_DIM) * (HEAD_DIM ** -0.5 * LOG2_E)).reshape(1, LANES)
    gk2 = jnp.tile(g_k, LANES // HEAD_DIM).reshape(1, LANES)
    qt, k, vt, u = _in_call(x, mod[:, 0:1], mod[:, 1:2], g_pre_mix.reshape(1, d), w_in, gq2, gk2,
                            cos_t, sin_t, tm_in)
    a = _attn_call(qt, k, vt, tq, tk)
    gains = jnp.stack([g_post_mix, g_pre_mlp, g_post_mlp])
    return _post_call(x, a, u, mod, gains, pool_scale.reshape(1, D_POOL), w_pool, w_out, w_ff1,
                      w_ff2, tm_post, 1024)


def kernel(x_prompt, x_sample, c_prompt, c_sample, w_ada, b_ada, g_pre_mix, g_post_mix,
           g_pre_mlp, g_post_mlp, w_in, g_q, g_k, w_pool, pool_scale, w_out, w_ff1, w_ff2):
    depth = w_ada.shape[0]
    bp, bs = c_prompt.shape[0], c_sample.shape[0]
    pad = (-(bp + bs)) % 8
    y_prompt, y_sample = x_prompt, x_sample
    for l in range(depth):
        c_all = jnp.concatenate([c_prompt, c_sample, jnp.zeros((pad, c_prompt.shape[1]), F32)])
        mod = _mod_call(c_all, w_ada[l], b_ada[l])
        weights = (g_pre_mix[l], g_post_mix[l], g_pre_mlp[l], g_post_mlp[l],
                   w_in[l].astype(BF16), g_q[l], g_k[l], w_pool[l].astype(BF16), pool_scale[l],
                   w_out[l].astype(BF16), w_ff1[l].astype(BF16), w_ff2[l].astype(BF16))
        y_prompt = _layer(y_prompt, mod[:bp], *weights)
        y_sample = _layer(y_sample, mod[bp:bp + bs], *weights)
    return (y_prompt, y_sample)
```

```python
import functools

import jax
import jax.numpy as jnp
from jax import lax
from jax.experimental import pallas as pl
from jax.experimental.pallas import tpu as pltpu

HEAD_DIM = 64
N_HEADS = 8
N_KV_HEADS = 2
GROUP = N_HEADS // N_KV_HEADS
D_Q = N_HEADS * HEAD_DIM
D_KV = N_KV_HEADS * HEAD_DIM
POOL_WINDOWS = (2, 4, 8, 16)
POOL_GROUP_DIM = 128
D_POOL = POOL_GROUP_DIM * len(POOL_WINDOWS)
GRID_W = 64
ROPE_THETA = 10000.0
EPS = 1e-6
N_MOD = 6
LOG2_E = 1.4426950408889634
LANES = 128
HALO = 16
LVL_ROW0 = HALO - max(POOL_WINDOWS) // 2
assert POOL_WINDOWS == tuple(2 ** (i + 1) for i in range(len(POOL_WINDOWS)))
V_ROWS = HEAD_DIM + 16
VMEM_LIMIT = 56 * 1024 * 1024

F32 = jnp.float32
BF16 = jnp.bfloat16


def _rms(x):
    return x * lax.rsqrt(jnp.mean(x * x, axis=-1, keepdims=True) + EPS)


def _mod_kernel(c_ref, w_ref, b_ref, o_ref):
    c = c_ref[...]
    sc = c / (1.0 + jnp.exp(-c))
    o_ref[...] = jnp.dot(sc, w_ref[...], precision=lax.Precision.HIGHEST,
                         preferred_element_type=F32) + b_ref[...]


def _mod_call(c, w_ada, b_ada):
    m, d = c.shape
    n = w_ada.shape[1]
    tn = 1024
    return pl.pallas_call(
        _mod_kernel,
        grid=(n // tn,),
        in_specs=[pl.BlockSpec((m, d), lambda j: (0, 0)),
                  pl.BlockSpec((d, tn), lambda j: (0, j)),
                  pl.BlockSpec((1, tn), lambda j: (0, j))],
        out_specs=pl.BlockSpec((m, tn), lambda j: (0, j)),
        out_shape=jax.ShapeDtypeStruct((m, n), F32),
        compiler_params=pltpu.CompilerParams(dimension_semantics=("arbitrary",),
                                             vmem_limit_bytes=VMEM_LIMIT),
        name="mod",
    )(c, w_ada, b_ada.reshape(1, n))


def _head_norm_rope(p, gain, cos_t, sin_t):
    lane = lax.broadcasted_iota(jnp.int32, p.shape, 1)
    lo = lane < HEAD_DIM
    sq = p * p
    s_lo = jnp.sum(jnp.where(lo, sq, 0.0), axis=-1, keepdims=True)
    s_hi = jnp.sum(jnp.where(lo, 0.0, sq), axis=-1, keepdims=True)
    ms = jnp.where(lo, s_lo, s_hi) * (1.0 / HEAD_DIM)
    pn = p * lax.rsqrt(ms + EPS) * gain
    fwd = pltpu.roll(pn, LANES - HEAD_DIM // 2, 1)
    bwd = pltpu.roll(pn, HEAD_DIM // 2, 1)
    rot = jnp.where((lane % HEAD_DIM) < HEAD_DIM // 2, fwd, bwd)
    return pn * cos_t + rot * sin_t


def _in_kernel(x_ref, shift_ref, scale_ref, g_ref, w_ref, gq_ref, gk_ref, cos_ref, sin_ref,
               qt_ref, k_ref, vt_ref, u_ref):
    x = x_ref[0]
    mult = g_ref[...] * (1.0 + scale_ref[0])
    h = _rms(x) * mult + shift_ref[0]
    proj = jnp.dot(h.astype(BF16), w_ref[...], preferred_element_type=F32)
    cos_t = cos_ref[...]
    sin_t = sin_ref[...]
    for g in range(GROUP):
        qg = _head_norm_rope(proj[:, g * LANES:(g + 1) * LANES], gq_ref[...], cos_t, sin_t)
        qt_ref[0, g] = qg.T.astype(BF16)
    k = _head_norm_rope(proj[:, D_Q:D_Q + D_KV], gk_ref[...], cos_t, sin_t)
    lane = lax.broadcasted_iota(jnp.int32, k.shape, 1)
    vt = proj[:, D_Q + D_KV:D_Q + 2 * D_KV].T.astype(BF16)
    for kv in range(N_KV_HEADS):
        own = (lane >= kv * HEAD_DIM) & (lane < (kv + 1) * HEAD_DIM)
        k_ref[0, kv] = jnp.where(own, k, 0.0).astype(BF16)
        vt_ref[0, kv, 0:HEAD_DIM, :] = vt[kv * HEAD_DIM:(kv + 1) * HEAD_DIM]
        vt_ref[0, kv, HEAD_DIM:, :] = jnp.ones((V_ROWS - HEAD_DIM, vt.shape[1]), BF16)
    u_ref[0] = proj[:, D_Q + 2 * D_KV:].astype(BF16)


def _in_call(x, shift, scale, g_pre, w_in, gq2, gk2, cos_t, sin_t, tm):
    b, s, d = x.shape
    d_in = w_in.shape[1]
    row = lambda i, j: (i, 0, 0)
    const2 = lambda i, j: (0, 0)
    return pl.pallas_call(
        _in_kernel,
        grid=(b, s // tm),
        in_specs=[pl.BlockSpec((1, tm, d), lambda i, j: (i, j, 0)),
                  pl.BlockSpec((1, 1, d), row),
                  pl.BlockSpec((1, 1, d), row),
                  pl.BlockSpec((1, d), const2),
                  pl.BlockSpec((d, d_in), const2),
                  pl.BlockSpec((1, LANES), const2),
                  pl.BlockSpec((1, LANES), const2),
                  pl.BlockSpec((tm, LANES), lambda i, j: (j, 0)),
                  pl.BlockSpec((tm, LANES), lambda i, j: (j, 0))],
        out_specs=[pl.BlockSpec((1, GROUP, LANES, tm), lambda i, j: (i, 0, 0, j)),
                   pl.BlockSpec((1, N_KV_HEADS, tm, D_KV), lambda i, j: (i, 0, j, 0)),
                   pl.BlockSpec((1, N_KV_HEADS, V_ROWS, tm), lambda i, j: (i, 0, 0, j)),
                   pl.BlockSpec((1, tm, D_POOL), lambda i, j: (i, j, 0))],
        out_shape=[jax.ShapeDtypeStruct((b, GROUP, LANES, s), BF16),
                   jax.ShapeDtypeStruct((b, N_KV_HEADS, s, D_KV), BF16),
                   jax.ShapeDtypeStruct((b, N_KV_HEADS, V_ROWS, s), BF16),
                   jax.ShapeDtypeStruct((b, s, D_POOL), BF16)],
        compiler_params=pltpu.CompilerParams(dimension_semantics=("arbitrary", "arbitrary"),
                                             vmem_limit_bytes=VMEM_LIMIT),
        name="in_proj",
    )(x, shift, scale, g_pre, w_in, gq2, gk2, cos_t, sin_t)


def _attn_kernel(qt_ref, k_ref, vt_ref, o_ref, s_scr, p_scr, acc_scr, *, tq, tk, nk, nq):
    mq = GROUP * tq
    n_flat = nq * nk
    nk_bits = nk.bit_length() - 1
    acc_scr[...] = jnp.zeros(acc_scr.shape, F32)

    def scores(f, slot):
        qoff = pl.multiple_of(lax.shift_right_logical(f, nk_bits) * tq, tq)
        koff = pl.multiple_of((f & (nk - 1)) * tk, tk)
        q_t = jnp.concatenate([qt_ref[0, g, :, pl.ds(qoff, tq)] for g in range(GROUP)], axis=1)
        s_t = jnp.dot(k_ref[0, 0, pl.ds(koff, tk), :], q_t, preferred_element_type=F32)
        s_scr[slot] = s_t
        return jnp.max(s_t, axis=0, keepdims=True)

    def numerator(f, slot, cmax, m_prev):
        m_prev = jnp.where((f & (nk - 1)) == 0, -jnp.inf, m_prev)
        m_new = jnp.maximum(m_prev, cmax)
        p_scr[slot] = jnp.exp2((s_scr[slot] - m_new).astype(BF16))
        return m_new, jnp.exp2(m_prev - m_new)

    def values(f, slot, alpha):
        koff = pl.multiple_of((f & (nk - 1)) * tk, tk)
        pv = jnp.dot(vt_ref[0, 0, :, pl.ds(koff, tk)], p_scr[slot], preferred_element_type=F32)
        acc_scr[...] = alpha * acc_scr[...] + pv

    def finalize(tile):
        acc = acc_scr[...]
        o_t = acc[0:HEAD_DIM] / acc[HEAD_DIM:HEAD_DIM + 1]
        qoff = pl.multiple_of(tile * tq, tq)
        o_ref[0, pl.ds(qoff, tq), :] = jnp.concatenate(
            [o_t[:, g * tq:(g + 1) * tq].T for g in range(GROUP)], axis=1).astype(BF16)

    m = jnp.full((1, mq), -jnp.inf, F32)
    cmax = scores(0, 0)
    cmax_next = scores(1, 1)
    m, alpha = numerator(0, 0, cmax, m)
    cmax = cmax_next

    def body(t, carry):
        cmax, m, alpha = carry
        f = 2 * t + 2
        for par in range(2):
            cmax_next = scores(f + par, par)
            m, alpha_next = numerator(f + par - 1, 1 - par, cmax, m)
            values(f + par - 2, par, alpha)
            cmax, alpha = cmax_next, alpha_next

        @pl.when((f & (nk - 1)) == 0)
        def _():
            finalize(lax.shift_right_logical(f, nk_bits) - 1)

        return cmax, m, alpha

    cmax, m, alpha = lax.fori_loop(0, (n_flat - 2) // 2, body, (cmax, m, alpha))
    m, alpha_last = numerator(n_flat - 1, 1, cmax, m)
    values(n_flat - 2, 0, alpha)
    values(n_flat - 1, 1, alpha_last)
    finalize(nq - 1)


def _attn_call(qt, k, vt, tq, tk, nq):
    b, _, _, s = qt.shape
    nk = s // tk
    assert nk >= 2 and nk & (nk - 1) == 0 and s % (nq * tq) == 0
    kern = functools.partial(_attn_kernel, tq=tq, tk=tk, nk=nk, nq=nq)
    gw = GROUP * HEAD_DIM
    mq = GROUP * tq
    return pl.pallas_call(
        kern,
        grid=(b, N_KV_HEADS, s // (nq * tq)),
        in_specs=[pl.BlockSpec((1, GROUP, LANES, nq * tq), lambda i, h, j: (i, 0, 0, j)),
                  pl.BlockSpec((1, 1, s, D_KV), lambda i, h, j: (i, h, 0, 0)),
                  pl.BlockSpec((1, 1, V_ROWS, s), lambda i, h, j: (i, h, 0, 0))],
        out_specs=pl.BlockSpec((1, nq * tq, gw), lambda i, h, j: (i, j, h)),
        out_shape=jax.ShapeDtypeStruct((b, s, D_Q), BF16),
        scratch_shapes=[pltpu.VMEM((2, tk, mq), F32),
                        pltpu.VMEM((2, tk, mq), BF16),
                        pltpu.VMEM((V_ROWS, mq), F32)],
        compiler_params=pltpu.CompilerParams(
            dimension_semantics=("arbitrary", "arbitrary", "arbitrary"),
            vmem_limit_bytes=VMEM_LIMIT),
        name="attention",
    )(qt, k, vt)


def _post_kernel(x_ref, a_ref, u_ref, up_ref, un_ref, mod_ref, gains_ref, ps_ref,
                 wp_ref, wo_ref, w1_ref, w2_ref, y_ref, ext_scr, *lvl_scrs, tm, seq, ff_chunk):
    j = pl.program_id(1)
    nj = pl.num_programs(1)
    u_main = u_ref[0].astype(F32)
    ext_scr[0:HALO, :] = jnp.where(j > 0, up_ref[0].astype(F32), 0.0)
    ext_scr[HALO:HALO + tm, :] = u_main
    ext_scr[HALO + tm:, :] = jnp.where(j < nj - 1, un_ref[0].astype(F32), 0.0)
    t = j * tm + lax.broadcasted_iota(jnp.int32, (tm, 1), 0)
    pooled = []
    src, first = ext_scr, 0
    for gi, w in enumerate(POOL_WINDOWS):
        cols = slice(gi * POOL_GROUP_DIM, (gi + 1) * POOL_GROUP_DIM)
        half = w // 2
        c0 = (gi - first) * POOL_GROUP_DIM
        own = slice(c0, c0 + POOL_GROUP_DIM)
        acc = src[HALO - half:HALO - half + tm, own] + src[HALO:HALO + tm, own]
        if gi + 1 < len(POOL_WINDOWS):
            dst = lvl_scrs[gi]
            rows = tm + 2 * HALO - 4 * (gi + 2) - LVL_ROW0
            dst[LVL_ROW0:LVL_ROW0 + rows, :] = (
                src[LVL_ROW0:LVL_ROW0 + rows, c0 + POOL_GROUP_DIM:]
                + src[LVL_ROW0 + half:LVL_ROW0 + half + rows, c0 + POOL_GROUP_DIM:])
            src, first = dst, gi + 1
        lo = jnp.clip(t - w // 2, 0, seq)
        hi = jnp.clip(t - w // 2 + w, 0, seq)
        cnt = (hi - lo).astype(F32)
        mixed = (acc / cnt - u_main[:, cols]).astype(BF16)
        pg = jnp.dot(mixed, wp_ref[gi], preferred_element_type=F32)
        pooled.append(pg * ps_ref[:, cols])
    p = jnp.concatenate(pooled, axis=1).astype(BF16)

    mix = (jnp.dot(a_ref[0], wo_ref[0:D_Q, :], preferred_element_type=F32)
           + jnp.dot(p, wo_ref[D_Q:, :], preferred_element_type=F32))
    gate_a = mod_ref[0, 2:3, :]
    shift_m = mod_ref[0, 3:4, :]
    scale_m = mod_ref[0, 4:5, :]
    gate_m = mod_ref[0, 5:6, :]
    x1 = x_ref[0] + gate_a * (_rms(mix) * gains_ref[0:1, :])
    h = (_rms(x1) * (gains_ref[1:2, :] * (1.0 + scale_m)) + shift_m).astype(BF16)
    d_ff = w1_ref.shape[1]
    f = jnp.zeros(x1.shape, F32)
    for c in range(d_ff // ff_chunk):
        hc = jnp.dot(h, w1_ref[:, c * ff_chunk:(c + 1) * ff_chunk], preferred_element_type=F32)
        hc = jnp.square(jnp.maximum(hc, 0.0)).astype(BF16)
        f = f + jnp.dot(hc, w2_ref[c * ff_chunk:(c + 1) * ff_chunk, :], preferred_element_type=F32)
    y_ref[0] = x1 + gate_m * (_rms(f) * gains_ref[2:3, :])


def _post_call(x, a, u, mod, gains, pool_scale, w_pool, w_out, w_ff1, w_ff2, tm, ff_chunk):
    b, s, d = x.shape
    d_ff = w_ff1.shape[1]
    nh = tm // HALO
    n_halo = s // HALO
    kern = functools.partial(_post_kernel, tm=tm, seq=s, ff_chunk=ff_chunk)
    const2 = lambda i, j: (0, 0)
    const3 = lambda i, j: (0, 0, 0)
    return pl.pallas_call(
        kern,
        grid=(b, s // tm),
        in_specs=[pl.BlockSpec((1, tm, d), lambda i, j: (i, j, 0)),
                  pl.BlockSpec((1, tm, D_Q), lambda i, j: (i, j, 0)),
                  pl.BlockSpec((1, tm, D_POOL), lambda i, j: (i, j, 0)),
                  pl.BlockSpec((1, HALO, D_POOL), lambda i, j: (i, jnp.maximum(j * nh - 1, 0), 0)),
                  pl.BlockSpec((1, HALO, D_POOL),
                               lambda i, j: (i, jnp.minimum((j + 1) * nh, n_halo - 1), 0)),
                  pl.BlockSpec((1, N_MOD, d), lambda i, j: (i, 0, 0)),
                  pl.BlockSpec((3, d), const2),
                  pl.BlockSpec((1, D_POOL), const2),
                  pl.BlockSpec(w_pool.shape, const3),
                  pl.BlockSpec(w_out.shape, const2, pipeline_mode=pl.Buffered(1)),
                  pl.BlockSpec((d, d_ff), const2, pipeline_mode=pl.Buffered(1)),
                  pl.BlockSpec((d_ff, d), const2, pipeline_mode=pl.Buffered(1))],
        out_specs=pl.BlockSpec((1, tm, d), lambda i, j: (i, j, 0)),
        out_shape=jax.ShapeDtypeStruct((b, s, d), F32),
        scratch_shapes=[pltpu.VMEM((tm + 2 * HALO, D_POOL - k * POOL_GROUP_DIM), F32)
                        for k in range(len(POOL_WINDOWS))],
        compiler_params=pltpu.CompilerParams(dimension_semantics=("arbitrary", "arbitrary"),
                                             vmem_limit_bytes=VMEM_LIMIT),
        name="post_mlp",
    )(x, a, u, u, u, mod, gains, pool_scale, w_pool, w_out, w_ff1, w_ff2)


def _rope_tables(seq_len):
    rows = seq_len // GRID_W
    row_idx = jnp.repeat(jnp.arange(rows, dtype=F32), GRID_W)
    col_idx = jnp.tile(jnp.arange(GRID_W, dtype=F32), rows)
    n_freq = HEAD_DIM // 4
    inv_freq = 1.0 / (ROPE_THETA ** (jnp.arange(n_freq, dtype=F32) / n_freq))
    ang = jnp.concatenate([row_idx[:, None] * inv_freq, col_idx[:, None] * inv_freq], axis=-1)
    cos, sin = jnp.cos(ang), jnp.sin(ang)
    cos_t = jnp.tile(cos, (1, LANES // (HEAD_DIM // 2)))
    sin_t = jnp.tile(jnp.concatenate([-sin, sin], axis=-1), (1, LANES // HEAD_DIM))
    return cos_t, sin_t


def _tiles(s):
    tm_in = min(512, s)
    tm_post = min(512, s)
    tq = min(256, s)
    tk = min(1024, s // 2)
    nq = min(8, s // tq)
    return tm_in, tm_post, tq, tk, nq


def _layer(x, mod, g_pre_mix, g_post_mix, g_pre_mlp, g_post_mlp, w_in, g_q, g_k, w_pool,
           pool_scale, w_out, w_ff1, w_ff2):
    b, s, d = x.shape
    tm_in, tm_post, tq, tk, nq = _tiles(s)
    mod = mod.reshape(b, N_MOD, d)
    cos_t, sin_t = _rope_tables(s)
    gq2 = (jnp.tile(g_q, LANES // HEAD_DIM) * (HEAD_DIM ** -0.5 * LOG2_E)).reshape(1, LANES)
    gk2 = jnp.tile(g_k, LANES // HEAD_DIM).reshape(1, LANES)
    qt, k, vt, u = _in_call(x, mod[:, 0:1], mod[:, 1:2], g_pre_mix.reshape(1, d), w_in, gq2, gk2,
                            cos_t, sin_t, tm_in)
    a = _attn_call(qt, k, vt, tq, tk, nq)
    gains = jnp.stack([g_post_mix, g_pre_mlp, g_post_mlp])
    return _post_call(x, a, u, mod, gains, pool_scale.reshape(1, D_POOL), w_pool, w_out, w_ff1,
                      w_ff2, tm_post, 1024)


def kernel(x_prompt, x_sample, c_prompt, c_sample, w_ada, b_ada, g_pre_mix, g_post_mix,
           g_pre_mlp, g_post_mlp, w_in, g_q, g_k, w_pool, pool_scale, w_out, w_ff1, w_ff2):
    depth = w_ada.shape[0]
    bp, bs = c_prompt.shape[0], c_sample.shape[0]
    pad = (-(bp + bs)) % 8
    y_prompt, y_sample = x_prompt, x_sample
    for l in range(depth):
        c_all = jnp.concatenate([c_prompt, c_sample, jnp.zeros((pad, c_prompt.shape[1]), F32)])
        mod = _mod_call(c_all, w_ada[l], b_ada[l])
        w_in_l = w_in[l]
        w_q = w_in_l[:, :D_Q].reshape(-1, N_KV_HEADS, GROUP, HEAD_DIM).transpose(0, 2, 1, 3)
        w_in_l = jnp.concatenate([w_q.reshape(-1, D_Q), w_in_l[:, D_Q:]], axis=1)
        weights = (g_pre_mix[l], g_post_mix[l], g_pre_mlp[l], g_post_mlp[l],
                   w_in_l.astype(BF16), g_q[l], g_k[l], w_pool[l].astype(BF16), pool_scale[l],
                   w_out[l].astype(BF16), w_ff1[l].astype(BF16), w_ff2[l].astype(BF16))
        y_prompt = _layer(y_prompt, mod[:bp], *weights)
        y_sample = _layer(y_sample, mod[bp:bp + bs], *weights)
    return (y_prompt, y_sample)
```

```python
import functools

import jax
import jax.numpy as jnp
from jax import lax
from jax.experimental import pallas as pl
from jax.experimental.pallas import tpu as pltpu

HEAD_DIM = 64
N_HEADS = 8
N_KV_HEADS = 2
GROUP = N_HEADS // N_KV_HEADS
D_Q = N_HEADS * HEAD_DIM
D_KV = N_KV_HEADS * HEAD_DIM
POOL_WINDOWS = (2, 4, 8, 16)
POOL_GROUP_DIM = 128
D_POOL = POOL_GROUP_DIM * len(POOL_WINDOWS)
GRID_W = 64
ROPE_THETA = 10000.0
EPS = 1e-6
N_MOD = 6
LOG2_E = 1.4426950408889634
LANES = 128
HALO = 16
LVL_ROW0 = HALO - max(POOL_WINDOWS) // 2
assert POOL_WINDOWS == tuple(2 ** (i + 1) for i in range(len(POOL_WINDOWS)))
V_ROWS = HEAD_DIM + 16
VMEM_LIMIT = 56 * 1024 * 1024

F32 = jnp.float32
BF16 = jnp.bfloat16


def _rms(x):
    return x * lax.rsqrt(jnp.mean(x * x, axis=-1, keepdims=True) + EPS)


def _mod_kernel(c_ref, w_ref, b_ref, o_ref):
    c = c_ref[...]
    sc = c / (1.0 + jnp.exp(-c))
    o_ref[...] = jnp.dot(sc, w_ref[...], precision=lax.Precision.HIGHEST,
                         preferred_element_type=F32) + b_ref[...]


def _mod_call(c, w_ada, b_ada):
    m, d = c.shape
    n = w_ada.shape[1]
    tn = 1024
    return pl.pallas_call(
        _mod_kernel,
        grid=(n // tn,),
        in_specs=[pl.BlockSpec((m, d), lambda j: (0, 0)),
                  pl.BlockSpec((d, tn), lambda j: (0, j)),
                  pl.BlockSpec((1, tn), lambda j: (0, j))],
        out_specs=pl.BlockSpec((m, tn), lambda j: (0, j)),
        out_shape=jax.ShapeDtypeStruct((m, n), F32),
        compiler_params=pltpu.CompilerParams(dimension_semantics=("arbitrary",),
                                             vmem_limit_bytes=VMEM_LIMIT),
        name="mod",
    )(c, w_ada, b_ada.reshape(1, n))


def _head_norm_rope(p, gain, cos_t, sin_t, head_ones):
    lane = lax.broadcasted_iota(jnp.int32, p.shape, 1)
    sq = p * p
    sq_hi = sq.astype(BF16)
    sq_lo = (sq - sq_hi.astype(F32)).astype(BF16)
    ms = (jnp.dot(sq_hi, head_ones, preferred_element_type=F32)
          + jnp.dot(sq_lo, head_ones, preferred_element_type=F32)) * (1.0 / HEAD_DIM)
    pn = p * lax.rsqrt(ms + EPS) * gain
    fwd = pltpu.roll(pn, LANES - HEAD_DIM // 2, 1)
    bwd = pltpu.roll(pn, HEAD_DIM // 2, 1)
    rot = jnp.where((lane % HEAD_DIM) < HEAD_DIM // 2, fwd, bwd)
    return pn * cos_t + rot * sin_t


def _in_kernel(x_ref, shift_ref, scale_ref, g_ref, w_ref, gq_ref, gk_ref, cos_ref, sin_ref,
               ones_ref, qt_ref, k_ref, vt_ref, u_ref):
    x = x_ref[0]
    mult = g_ref[...] * (1.0 + scale_ref[0])
    h = _rms(x) * mult + shift_ref[0]
    proj = jnp.dot(h.astype(BF16), w_ref[...], preferred_element_type=F32)
    cos_t = cos_ref[...]
    sin_t = sin_ref[...]
    for g in range(GROUP):
        qg = _head_norm_rope(proj[:, g * LANES:(g + 1) * LANES], gq_ref[...], cos_t, sin_t,
                             ones_ref[...])
        qt_ref[0, g] = qg.astype(BF16).T
    k = _head_norm_rope(proj[:, D_Q:D_Q + D_KV], gk_ref[...], cos_t, sin_t, ones_ref[...])
    lane = lax.broadcasted_iota(jnp.int32, k.shape, 1)
    vt = proj[:, D_Q + D_KV:D_Q + 2 * D_KV].astype(BF16).T
    for kv in range(N_KV_HEADS):
        own = (lane >= kv * HEAD_DIM) & (lane < (kv + 1) * HEAD_DIM)
        k_ref[0, kv] = jnp.where(own, k, 0.0).astype(BF16)
        vt_ref[0, kv, 0:HEAD_DIM, :] = vt[kv * HEAD_DIM:(kv + 1) * HEAD_DIM]
        vt_ref[0, kv, HEAD_DIM:, :] = jnp.ones((V_ROWS - HEAD_DIM, vt.shape[1]), BF16)
    u_ref[0] = proj[:, D_Q + 2 * D_KV:].astype(BF16)


def _in_call(x, shift, scale, g_pre, w_in, gq2, gk2, cos_t, sin_t, tm):
    b, s, d = x.shape
    d_in = w_in.shape[1]
    head_id = jnp.arange(LANES) // HEAD_DIM
    head_ones = (head_id[:, None] == head_id[None, :]).astype(BF16)
    row = lambda i, j: (i, 0, 0)
    const2 = lambda i, j: (0, 0)
    return pl.pallas_call(
        _in_kernel,
        grid=(b, s // tm),
        in_specs=[pl.BlockSpec((1, tm, d), lambda i, j: (i, j, 0)),
                  pl.BlockSpec((1, 1, d), row),
                  pl.BlockSpec((1, 1, d), row),
                  pl.BlockSpec((1, d), const2),
                  pl.BlockSpec((d, d_in), const2),
                  pl.BlockSpec((1, LANES), const2),
                  pl.BlockSpec((1, LANES), const2),
                  pl.BlockSpec((tm, LANES), lambda i, j: (j, 0)),
                  pl.BlockSpec((tm, LANES), lambda i, j: (j, 0)),
                  pl.BlockSpec((LANES, LANES), const2)],
        out_specs=[pl.BlockSpec((1, GROUP, LANES, tm), lambda i, j: (i, 0, 0, j)),
                   pl.BlockSpec((1, N_KV_HEADS, tm, D_KV), lambda i, j: (i, 0, j, 0)),
                   pl.BlockSpec((1, N_KV_HEADS, V_ROWS, tm), lambda i, j: (i, 0, 0, j)),
                   pl.BlockSpec((1, tm, D_POOL), lambda i, j: (i, j, 0))],
        out_shape=[jax.ShapeDtypeStruct((b, GROUP, LANES, s), BF16),
                   jax.ShapeDtypeStruct((b, N_KV_HEADS, s, D_KV), BF16),
                   jax.ShapeDtypeStruct((b, N_KV_HEADS, V_ROWS, s), BF16),
                   jax.ShapeDtypeStruct((b, s, D_POOL), BF16)],
        compiler_params=pltpu.CompilerParams(dimension_semantics=("arbitrary", "arbitrary"),
                                             vmem_limit_bytes=VMEM_LIMIT),
        name="in_proj",
    )(x, shift, scale, g_pre, w_in, gq2, gk2, cos_t, sin_t, head_ones)


def _attn_kernel(qt_ref, k_ref, vt_ref, o_ref, s_scr, p_scr, acc_scr, *, tq, tk, nk, nq):
    mq = GROUP * tq
    n_flat = nq * nk
    nk_bits = nk.bit_length() - 1
    acc_scr[...] = jnp.zeros(acc_scr.shape, F32)

    def scores(f, slot):
        qoff = pl.multiple_of(lax.shift_right_logical(f, nk_bits) * tq, tq)
        koff = pl.multiple_of((f & (nk - 1)) * tk, tk)
        q_t = jnp.concatenate([qt_ref[0, g, :, pl.ds(qoff, tq)] for g in range(GROUP)], axis=1)
        s_t = jnp.dot(k_ref[0, 0, pl.ds(koff, tk), :], q_t, preferred_element_type=F32)
        s_scr[slot] = s_t
        return jnp.max(s_t, axis=0, keepdims=True)

    def numerator(f, slot, cmax, m_prev):
        m_prev = jnp.where((f & (nk - 1)) == 0, -jnp.inf, m_prev)
        m_new = jnp.maximum(m_prev, cmax)
        p_scr[slot] = jnp.exp2((s_scr[slot] - m_new).astype(BF16))
        return m_new, jnp.exp2(m_prev - m_new)

    def values(f, slot, alpha):
        koff = pl.multiple_of((f & (nk - 1)) * tk, tk)
        pv = jnp.dot(vt_ref[0, 0, :, pl.ds(koff, tk)], p_scr[slot], preferred_element_type=F32)
        acc_scr[...] = alpha * acc_scr[...] + pv

    def finalize(tile):
        acc = acc_scr[...]
        o_t = acc[0:HEAD_DIM] / acc[HEAD_DIM:HEAD_DIM + 1]
        qoff = pl.multiple_of(tile * tq, tq)
        o_ref[0, pl.ds(qoff, tq), :] = jnp.concatenate(
            [o_t[:, g * tq:(g + 1) * tq].T for g in range(GROUP)], axis=1).astype(BF16)

    m = jnp.full((1, mq), -jnp.inf, F32)
    cmax = scores(0, 0)
    cmax_next = scores(1, 1)
    m, alpha = numerator(0, 0, cmax, m)
    cmax = cmax_next

    def body(t, carry):
        cmax, m, alpha = carry
        f = 2 * t + 2
        for par in range(2):
            cmax_next = scores(f + par, par)
            m, alpha_next = numerator(f + par - 1, 1 - par, cmax, m)
            values(f + par - 2, par, alpha)
            cmax, alpha = cmax_next, alpha_next

        @pl.when((f & (nk - 1)) == 0)
        def _():
            finalize(lax.shift_right_logical(f, nk_bits) - 1)

        return cmax, m, alpha

    cmax, m, alpha = lax.fori_loop(0, (n_flat - 2) // 2, body, (cmax, m, alpha))
    m, alpha_last = numerator(n_flat - 1, 1, cmax, m)
    values(n_flat - 2, 0, alpha)
    values(n_flat - 1, 1, alpha_last)
    finalize(nq - 1)


def _attn_call(qt, k, vt, tq, tk, nq):
    b, _, _, s = qt.shape
    nk = s // tk
    assert nk >= 2 and nk & (nk - 1) == 0 and s % (nq * tq) == 0
    kern = functools.partial(_attn_kernel, tq=tq, tk=tk, nk=nk, nq=nq)
    gw = GROUP * HEAD_DIM
    mq = GROUP * tq
    return pl.pallas_call(
        kern,
        grid=(b, N_KV_HEADS, s // (nq * tq)),
        in_specs=[pl.BlockSpec((1, GROUP, LANES, nq * tq), lambda i, h, j: (i, 0, 0, j)),
                  pl.BlockSpec((1, 1, s, D_KV), lambda i, h, j: (i, h, 0, 0)),
                  pl.BlockSpec((1, 1, V_ROWS, s), lambda i, h, j: (i, h, 0, 0))],
        out_specs=pl.BlockSpec((1, nq * tq, gw), lambda i, h, j: (i, j, h)),
        out_shape=jax.ShapeDtypeStruct((b, s, D_Q), BF16),
        scratch_shapes=[pltpu.VMEM((2, tk, mq), F32),
                        pltpu.VMEM((2, tk, mq), BF16),
                        pltpu.VMEM((V_ROWS, mq), F32)],
        compiler_params=pltpu.CompilerParams(
            dimension_semantics=("arbitrary", "arbitrary", "arbitrary"),
            vmem_limit_bytes=VMEM_LIMIT),
        name="attention",
    )(qt, k, vt)


def _post_kernel(x_ref, a_ref, u_ref, up_ref, un_ref, mod_ref, gains_ref, ps_ref,
                 wp_ref, wo_ref, w1_ref, w2_ref, y_ref, ext_scr, *lvl_scrs, tm, seq, ff_chunk):
    j = pl.program_id(1)
    nj = pl.num_programs(1)
    u_main = u_ref[0].astype(F32)
    ext_scr[0:HALO, :] = jnp.where(j > 0, up_ref[0].astype(F32), 0.0)
    ext_scr[HALO:HALO + tm, :] = u_main
    ext_scr[HALO + tm:, :] = jnp.where(j < nj - 1, un_ref[0].astype(F32), 0.0)
    t = j * tm + lax.broadcasted_iota(jnp.int32, (tm, 1), 0)
    pooled = []
    src, first = ext_scr, 0
    for gi, w in enumerate(POOL_WINDOWS):
        cols = slice(gi * POOL_GROUP_DIM, (gi + 1) * POOL_GROUP_DIM)
        half = w // 2
        c0 = (gi - first) * POOL_GROUP_DIM
        own = slice(c0, c0 + POOL_GROUP_DIM)
        acc = src[HALO - half:HALO - half + tm, own] + src[HALO:HALO + tm, own]
        if gi + 1 < len(POOL_WINDOWS):
            dst = lvl_scrs[gi]
            rows = tm + 2 * HALO - 4 * (gi + 2) - LVL_ROW0
            dst[LVL_ROW0:LVL_ROW0 + rows, :] = (
                src[LVL_ROW0:LVL_ROW0 + rows, c0 + POOL_GROUP_DIM:]
                + src[LVL_ROW0 + half:LVL_ROW0 + half + rows, c0 + POOL_GROUP_DIM:])
            src, first = dst, gi + 1
        lo = jnp.clip(t - w // 2, 0, seq)
        hi = jnp.clip(t - w // 2 + w, 0, seq)
        cnt = (hi - lo).astype(F32)
        mixed = (acc / cnt - u_main[:, cols]).astype(BF16)
        pg = jnp.dot(mixed, wp_ref[gi], preferred_element_type=F32)
        pooled.append(pg * ps_ref[:, cols])
    p = jnp.concatenate(pooled, axis=1).astype(BF16)

    mix = (jnp.dot(a_ref[0], wo_ref[0:D_Q, :], preferred_element_type=F32)
           + jnp.dot(p, wo_ref[D_Q:, :], preferred_element_type=F32))
    gate_a = mod_ref[0, 2:3, :]
    shift_m = mod_ref[0, 3:4, :]
    scale_m = mod_ref[0, 4:5, :]
    gate_m = mod_ref[0, 5:6, :]
    x1 = x_ref[0] + gate_a * (_rms(mix) * gains_ref[0:1, :])
    h = (_rms(x1) * (gains_ref[1:2, :] * (1.0 + scale_m)) + shift_m).astype(BF16)
    d_ff = w1_ref.shape[1]
    f = jnp.zeros(x1.shape, F32)
    for c in range(d_ff // ff_chunk):
        hc = jnp.dot(h, w1_ref[:, c * ff_chunk:(c + 1) * ff_chunk], preferred_element_type=F32)
        hc = jnp.square(jnp.maximum(hc, 0.0)).astype(BF16)
        f = f + jnp.dot(hc, w2_ref[c * ff_chunk:(c + 1) * ff_chunk, :], preferred_element_type=F32)
    y_ref[0] = x1 + gate_m * (_rms(f) * gains_ref[2:3, :])


def _post_call(x, a, u, mod, gains, pool_scale, w_pool, w_out, w_ff1, w_ff2, tm, ff_chunk):
    b, s, d = x.shape
    d_ff = w_ff1.shape[1]
    nh = tm // HALO
    n_halo = s // HALO
    kern = functools.partial(_post_kernel, tm=tm, seq=s, ff_chunk=ff_chunk)
    const2 = lambda i, j: (0, 0)
    const3 = lambda i, j: (0, 0, 0)
    return pl.pallas_call(
        kern,
        grid=(b, s // tm),
        in_specs=[pl.BlockSpec((1, tm, d), lambda i, j: (i, j, 0)),
                  pl.BlockSpec((1, tm, D_Q), lambda i, j: (i, j, 0)),
                  pl.BlockSpec((1, tm, D_POOL), lambda i, j: (i, j, 0)),
                  pl.BlockSpec((1, HALO, D_POOL), lambda i, j: (i, jnp.maximum(j * nh - 1, 0), 0)),
                  pl.BlockSpec((1, HALO, D_POOL),
                               lambda i, j: (i, jnp.minimum((j + 1) * nh, n_halo - 1), 0)),
                  pl.BlockSpec((1, N_MOD, d), lambda i, j: (i, 0, 0)),
                  pl.BlockSpec((3, d), const2),
                  pl.BlockSpec((1, D_POOL), const2),
                  pl.BlockSpec(w_pool.shape, const3),
                  pl.BlockSpec(w_out.shape, const2, pipeline_mode=pl.Buffered(1)),
                  pl.BlockSpec((d, d_ff), const2, pipeline_mode=pl.Buffered(1)),
                  pl.BlockSpec((d_ff, d), const2, pipeline_mode=pl.Buffered(1))],
        out_specs=pl.BlockSpec((1, tm, d), lambda i, j: (i, j, 0)),
        out_shape=jax.ShapeDtypeStruct((b, s, d), F32),
        scratch_shapes=[pltpu.VMEM((tm + 2 * HALO, D_POOL - k * POOL_GROUP_DIM), F32)
                        for k in range(len(POOL_WINDOWS))],
        compiler_params=pltpu.CompilerParams(dimension_semantics=("arbitrary", "arbitrary"),
                                             vmem_limit_bytes=VMEM_LIMIT),
        name="post_mlp",
    )(x, a, u, u, u, mod, gains, pool_scale, w_pool, w_out, w_ff1, w_ff2)


def _rope_tables(seq_len):
    rows = seq_len // GRID_W
    row_idx = jnp.repeat(jnp.arange(rows, dtype=F32), GRID_W)
    col_idx = jnp.tile(jnp.arange(GRID_W, dtype=F32), rows)
    n_freq = HEAD_DIM // 4
    inv_freq = 1.0 / (ROPE_THETA ** (jnp.arange(n_freq, dtype=F32) / n_freq))
    ang = jnp.concatenate([row_idx[:, None] * inv_freq, col_idx[:, None] * inv_freq], axis=-1)
    cos, sin = jnp.cos(ang), jnp.sin(ang)
    cos_t = jnp.tile(cos, (1, LANES // (HEAD_DIM // 2)))
    sin_t = jnp.tile(jnp.concatenate([-sin, sin], axis=-1), (1, LANES // HEAD_DIM))
    return cos_t, sin_t


def _tiles(s):
    tm_in = min(512, s)
    tm_post = min(512, s)
    tq = min(256, s)
    tk = min(1024, s // 2)
    nq = min(16, s // tq)
    return tm_in, tm_post, tq, tk, nq


def _layer(x, mod, g_pre_mix, g_post_mix, g_pre_mlp, g_post_mlp, w_in, g_q, g_k, w_pool,
           pool_scale, w_out, w_ff1, w_ff2):
    b, s, d = x.shape
    tm_in, tm_post, tq, tk, nq = _tiles(s)
    mod = mod.reshape(b, N_MOD, d)
    cos_t, sin_t = _rope_tables(s)
    gq2 = (jnp.tile(g_q, LANES // HEAD_DIM) * (HEAD_DIM ** -0.5 * LOG2_E)).reshape(1, LANES)
    gk2 = jnp.tile(g_k, LANES // HEAD_DIM).reshape(1, LANES)
    qt, k, vt, u = _in_call(x, mod[:, 0:1], mod[:, 1:2], g_pre_mix.reshape(1, d), w_in, gq2, gk2,
                            cos_t, sin_t, tm_in)
    a = _attn_call(qt, k, vt, tq, tk, nq)
    gains = jnp.stack([g_post_mix, g_pre_mlp, g_post_mlp])
    return _post_call(x, a, u, mod, gains, pool_scale.reshape(1, D_POOL), w_pool, w_out, w_ff1,
                      w_ff2, tm_post, 1024)


def kernel(x_prompt, x_sample, c_prompt, c_sample, w_ada, b_ada, g_pre_mix, g_post_mix,
           g_pre_mlp, g_post_mlp, w_in, g_q, g_k, w_pool, pool_scale, w_out, w_ff1, w_ff2):
    depth = w_ada.shape[0]
    bp, bs = c_prompt.shape[0], c_sample.shape[0]
    pad = (-(bp + bs)) % 8
    y_prompt, y_sample = x_prompt, x_sample
    for l in range(depth):
        c_all = jnp.concatenate([c_prompt, c_sample, jnp.zeros((pad, c_prompt.shape[1]), F32)])
        mod = _mod_call(c_all, w_ada[l], b_ada[l])
        w_in_l = w_in[l]
        w_q = w_in_l[:, :D_Q].reshape(-1, N_KV_HEADS, GROUP, HEAD_DIM).transpose(0, 2, 1, 3)
        w_in_l = jnp.concatenate([w_q.reshape(-1, D_Q), w_in_l[:, D_Q:]], axis=1)
        weights = (g_pre_mix[l], g_post_mix[l], g_pre_mlp[l], g_post_mlp[l],
                   w_in_l.astype(BF16), g_q[l], g_k[l], w_pool[l].astype(BF16), pool_scale[l],
                   w_out[l].astype(BF16), w_ff1[l].astype(BF16), w_ff2[l].astype(BF16))
        y_prompt = _layer(y_prompt, mod[:bp], *weights)
        y_sample = _layer(y_sample, mod[bp:bp + bs], *weights)
    return (y_prompt, y_sample)
```

```python
import functools

import jax
import jax.numpy as jnp
from jax import lax
from jax.experimental import pallas as pl
from jax.experimental.pallas import tpu as pltpu

HEAD_DIM = 64
N_HEADS = 8
N_KV_HEADS = 2
GROUP = N_HEADS // N_KV_HEADS
D_Q = N_HEADS * HEAD_DIM
D_KV = N_KV_HEADS * HEAD_DIM
POOL_WINDOWS = (2, 4, 8, 16)
POOL_GROUP_DIM = 128
D_POOL = POOL_GROUP_DIM * len(POOL_WINDOWS)
GRID_W = 64
ROPE_THETA = 10000.0
EPS = 1e-6
N_MOD = 6
LOG2_E = 1.4426950408889634
LANES = 128
HALO = 16
LVL_ROW0 = HALO - max(POOL_WINDOWS) // 2
assert POOL_WINDOWS == tuple(2 ** (i + 1) for i in range(len(POOL_WINDOWS)))
V_ROWS = HEAD_DIM + 16
VMEM_LIMIT = 56 * 1024 * 1024

F32 = jnp.float32
BF16 = jnp.bfloat16


def _rms(x):
    return x * lax.rsqrt(jnp.mean(x * x, axis=-1, keepdims=True) + EPS)


def _mod_kernel(c_ref, w_ref, b_ref, o_ref):
    c = c_ref[...]
    sc = c / (1.0 + jnp.exp(-c))
    o_ref[...] = jnp.dot(sc, w_ref[...], precision=lax.Precision.HIGHEST,
                         preferred_element_type=F32) + b_ref[...]


def _mod_call(c, w_ada, b_ada):
    m, d = c.shape
    n = w_ada.shape[1]
    tn = 1024
    return pl.pallas_call(
        _mod_kernel,
        grid=(n // tn,),
        in_specs=[pl.BlockSpec((m, d), lambda j: (0, 0)),
                  pl.BlockSpec((d, tn), lambda j: (0, j)),
                  pl.BlockSpec((1, tn), lambda j: (0, j))],
        out_specs=pl.BlockSpec((m, tn), lambda j: (0, j)),
        out_shape=jax.ShapeDtypeStruct((m, n), F32),
        compiler_params=pltpu.CompilerParams(dimension_semantics=("arbitrary",),
                                             vmem_limit_bytes=VMEM_LIMIT),
        name="mod",
    )(c, w_ada, b_ada.reshape(1, n))


def _head_norm_rope(p, gain, cos_t, sin_t, head_ones):
    lane = lax.broadcasted_iota(jnp.int32, p.shape, 1)
    sq = p * p
    sq_hi = sq.astype(BF16)
    sq_lo = (sq - sq_hi.astype(F32)).astype(BF16)
    ms = (jnp.dot(sq_hi, head_ones, preferred_element_type=F32)
          + jnp.dot(sq_lo, head_ones, preferred_element_type=F32)) * (1.0 / HEAD_DIM)
    pn = p * lax.rsqrt(ms + EPS) * gain
    fwd = pltpu.roll(pn, LANES - HEAD_DIM // 2, 1)
    bwd = pltpu.roll(pn, HEAD_DIM // 2, 1)
    rot = jnp.where((lane % HEAD_DIM) < HEAD_DIM // 2, fwd, bwd)
    return pn * cos_t + rot * sin_t


def _in_kernel(x_ref, shift_ref, scale_ref, g_ref, w_ref, gq_ref, gk_ref, cos_ref, sin_ref,
               ones_ref, qt_ref, k_ref, vt_ref, u_ref):
    x = x_ref[0]
    mult = g_ref[...] * (1.0 + scale_ref[0])
    h = _rms(x) * mult + shift_ref[0]
    proj = jnp.dot(h.astype(BF16), w_ref[...], preferred_element_type=F32)
    cos_t = cos_ref[...]
    sin_t = sin_ref[...]
    for g in range(GROUP):
        qg = _head_norm_rope(proj[:, g * LANES:(g + 1) * LANES], gq_ref[...], cos_t, sin_t,
                             ones_ref[...])
        qt_ref[0, g] = qg.astype(BF16).T
    k = _head_norm_rope(proj[:, D_Q:D_Q + D_KV], gk_ref[...], cos_t, sin_t, ones_ref[...])
    lane = lax.broadcasted_iota(jnp.int32, k.shape, 1)
    vt = proj[:, D_Q + D_KV:D_Q + 2 * D_KV].astype(BF16).T
    for kv in range(N_KV_HEADS):
        own = (lane >= kv * HEAD_DIM) & (lane < (kv + 1) * HEAD_DIM)
        k_ref[0, kv] = jnp.where(own, k, 0.0).astype(BF16)
        vt_ref[0, kv, 0:HEAD_DIM, :] = vt[kv * HEAD_DIM:(kv + 1) * HEAD_DIM]
        vt_ref[0, kv, HEAD_DIM:, :] = jnp.ones((V_ROWS - HEAD_DIM, vt.shape[1]), BF16)
    u_ref[0] = proj[:, D_Q + 2 * D_KV:].astype(BF16)


def _in_call(x, shift, scale, g_pre, w_in, gq2, gk2, cos_t, sin_t, tm):
    b, s, d = x.shape
    d_in = w_in.shape[1]
    head_id = jnp.arange(LANES) // HEAD_DIM
    head_ones = (head_id[:, None] == head_id[None, :]).astype(BF16)
    row = lambda i, j: (i, 0, 0)
    const2 = lambda i, j: (0, 0)
    return pl.pallas_call(
        _in_kernel,
        grid=(b, s // tm),
        in_specs=[pl.BlockSpec((1, tm, d), lambda i, j: (i, j, 0)),
                  pl.BlockSpec((1, 1, d), row),
                  pl.BlockSpec((1, 1, d), row),
                  pl.BlockSpec((1, d), const2),
                  pl.BlockSpec((d, d_in), const2),
                  pl.BlockSpec((1, LANES), const2),
                  pl.BlockSpec((1, LANES), const2),
                  pl.BlockSpec((tm, LANES), lambda i, j: (j, 0)),
                  pl.BlockSpec((tm, LANES), lambda i, j: (j, 0)),
                  pl.BlockSpec((LANES, LANES), const2)],
        out_specs=[pl.BlockSpec((1, GROUP, LANES, tm), lambda i, j: (i, 0, 0, j)),
                   pl.BlockSpec((1, N_KV_HEADS, tm, D_KV), lambda i, j: (i, 0, j, 0)),
                   pl.BlockSpec((1, N_KV_HEADS, V_ROWS, tm), lambda i, j: (i, 0, 0, j)),
                   pl.BlockSpec((1, tm, D_POOL), lambda i, j: (i, j, 0))],
        out_shape=[jax.ShapeDtypeStruct((b, GROUP, LANES, s), BF16),
                   jax.ShapeDtypeStruct((b, N_KV_HEADS, s, D_KV), BF16),
                   jax.ShapeDtypeStruct((b, N_KV_HEADS, V_ROWS, s), BF16),
                   jax.ShapeDtypeStruct((b, s, D_POOL), BF16)],
        compiler_params=pltpu.CompilerParams(dimension_semantics=("arbitrary", "arbitrary"),
                                             vmem_limit_bytes=VMEM_LIMIT),
        name="in_proj",
    )(x, shift, scale, g_pre, w_in, gq2, gk2, cos_t, sin_t, head_ones)


def _attn_kernel(qt_ref, k_ref, vt_ref, o_ref, s_scr, p_scr, acc_scr, *, tq, tk, nk, nq):
    mq = GROUP * tq
    n_flat = nq * nk
    nk_bits = nk.bit_length() - 1
    acc_scr[...] = jnp.zeros(acc_scr.shape, F32)

    def scores(f, slot):
        qoff = pl.multiple_of(lax.shift_right_logical(f, nk_bits) * tq, tq)
        koff = pl.multiple_of((f & (nk - 1)) * tk, tk)
        q_t = jnp.concatenate([qt_ref[0, g, :, pl.ds(qoff, tq)] for g in range(GROUP)], axis=1)
        s_t = jnp.dot(k_ref[0, 0, pl.ds(koff, tk), :], q_t, preferred_element_type=F32)
        s_b = s_t.astype(BF16)
        s_scr[slot] = s_b
        return jnp.max(s_b, axis=0, keepdims=True).astype(F32)

    def numerator(f, slot, cmax, m_prev):
        m_prev = jnp.where((f & (nk - 1)) == 0, -jnp.inf, m_prev)
        m_new = jnp.maximum(m_prev, cmax)
        p_scr[slot] = jnp.exp2(s_scr[slot] - m_new.astype(BF16))
        return m_new, jnp.exp2(m_prev - m_new)

    def values(f, slot, alpha):
        koff = pl.multiple_of((f & (nk - 1)) * tk, tk)
        pv = jnp.dot(vt_ref[0, 0, :, pl.ds(koff, tk)], p_scr[slot], preferred_element_type=F32)
        acc_scr[...] = alpha * acc_scr[...] + pv

    def finalize(tile):
        acc = acc_scr[...]
        o_t = acc[0:HEAD_DIM] / acc[HEAD_DIM:HEAD_DIM + 1]
        qoff = pl.multiple_of(tile * tq, tq)
        o_ref[0, pl.ds(qoff, tq), :] = jnp.concatenate(
            [o_t[:, g * tq:(g + 1) * tq].T for g in range(GROUP)], axis=1).astype(BF16)

    m = jnp.full((1, mq), -jnp.inf, F32)
    cmax = scores(0, 0)
    cmax_next = scores(1, 1)
    m, alpha = numerator(0, 0, cmax, m)
    cmax = cmax_next

    def body(t, carry):
        cmax, m, alpha = carry
        f = 2 * t + 2
        for par in range(2):
            cmax_next = scores(f + par, par)
            m, alpha_next = numerator(f + par - 1, 1 - par, cmax, m)
            values(f + par - 2, par, alpha)
            cmax, alpha = cmax_next, alpha_next

        @pl.when((f & (nk - 1)) == 0)
        def _():
            finalize(lax.shift_right_logical(f, nk_bits) - 1)

        return cmax, m, alpha

    cmax, m, alpha = lax.fori_loop(0, (n_flat - 2) // 2, body, (cmax, m, alpha))
    m, alpha_last = numerator(n_flat - 1, 1, cmax, m)
    values(n_flat - 2, 0, alpha)
    values(n_flat - 1, 1, alpha_last)
    finalize(nq - 1)


def _attn_call(qt, k, vt, tq, tk, nq):
    b, _, _, s = qt.shape
    nk = s // tk
    assert nk >= 2 and nk & (nk - 1) == 0 and s % (nq * tq) == 0
    kern = functools.partial(_attn_kernel, tq=tq, tk=tk, nk=nk, nq=nq)
    gw = GROUP * HEAD_DIM
    mq = GROUP * tq
    return pl.pallas_call(
        kern,
        grid=(b, N_KV_HEADS, s // (nq * tq)),
        in_specs=[pl.BlockSpec((1, GROUP, LANES, nq * tq), lambda i, h, j: (i, 0, 0, j)),
                  pl.BlockSpec((1, 1, s, D_KV), lambda i, h, j: (i, h, 0, 0)),
                  pl.BlockSpec((1, 1, V_ROWS, s), lambda i, h, j: (i, h, 0, 0))],
        out_specs=pl.BlockSpec((1, nq * tq, gw), lambda i, h, j: (i, j, h)),
        out_shape=jax.ShapeDtypeStruct((b, s, D_Q), BF16),
        scratch_shapes=[pltpu.VMEM((2, tk, mq), BF16),
                        pltpu.VMEM((2, tk, mq), BF16),
                        pltpu.VMEM((V_ROWS, mq), F32)],
        compiler_params=pltpu.CompilerParams(
            dimension_semantics=("arbitrary", "arbitrary", "arbitrary"),
            vmem_limit_bytes=VMEM_LIMIT),
        name="attention",
    )(qt, k, vt)


def _post_kernel(x_ref, a_ref, u_ref, up_ref, un_ref, mod_ref, gains_ref, ps_ref,
                 wp_ref, wo_ref, w1_ref, w2_ref, y_ref, ext_scr, *lvl_scrs, tm, seq, ff_chunk):
    j = pl.program_id(1)
    nj = pl.num_programs(1)
    u_main = u_ref[0].astype(F32)
    ext_scr[0:HALO, :] = jnp.where(j > 0, up_ref[0].astype(F32), 0.0)
    ext_scr[HALO:HALO + tm, :] = u_main
    ext_scr[HALO + tm:, :] = jnp.where(j < nj - 1, un_ref[0].astype(F32), 0.0)
    t = j * tm + lax.broadcasted_iota(jnp.int32, (tm, 1), 0)
    pooled = []
    src, first = ext_scr, 0
    for gi, w in enumerate(POOL_WINDOWS):
        cols = slice(gi * POOL_GROUP_DIM, (gi + 1) * POOL_GROUP_DIM)
        half = w // 2
        c0 = (gi - first) * POOL_GROUP_DIM
        own = slice(c0, c0 + POOL_GROUP_DIM)
        acc = src[HALO - half:HALO - half + tm, own] + src[HALO:HALO + tm, own]
        if gi + 1 < len(POOL_WINDOWS):
            dst = lvl_scrs[gi]
            rows = tm + 2 * HALO - 4 * (gi + 2) - LVL_ROW0
            dst[LVL_ROW0:LVL_ROW0 + rows, :] = (
                src[LVL_ROW0:LVL_ROW0 + rows, c0 + POOL_GROUP_DIM:]
                + src[LVL_ROW0 + half:LVL_ROW0 + half + rows, c0 + POOL_GROUP_DIM:])
            src, first = dst, gi + 1
        lo = jnp.clip(t - w // 2, 0, seq)
        hi = jnp.clip(t - w // 2 + w, 0, seq)
        cnt = (hi - lo).astype(F32)
        mixed = (acc / cnt - u_main[:, cols]).astype(BF16)
        pg = jnp.dot(mixed, wp_ref[gi], preferred_element_type=F32)
        pooled.append(pg * ps_ref[:, cols])
    p = jnp.concatenate(pooled, axis=1).astype(BF16)

    mix = (jnp.dot(a_ref[0], wo_ref[0:D_Q, :], preferred_element_type=F32)
           + jnp.dot(p, wo_ref[D_Q:, :], preferred_element_type=F32))
    gate_a = mod_ref[0, 2:3, :]
    shift_m = mod_ref[0, 3:4, :]
    scale_m = mod_ref[0, 4:5, :]
    gate_m = mod_ref[0, 5:6, :]
    x1 = x_ref[0] + gate_a * (_rms(mix) * gains_ref[0:1, :])
    h = (_rms(x1) * (gains_ref[1:2, :] * (1.0 + scale_m)) + shift_m).astype(BF16)
    d_ff = w1_ref.shape[1]
    f = jnp.zeros(x1.shape, F32)
    for c in range(d_ff // ff_chunk):
        hc = jnp.dot(h, w1_ref[:, c * ff_chunk:(c + 1) * ff_chunk], preferred_element_type=F32)
        hc = jnp.square(jnp.maximum(hc, 0.0)).astype(BF16)
        f = f + jnp.dot(hc, w2_ref[c * ff_chunk:(c + 1) * ff_chunk, :], preferred_element_type=F32)
    y_ref[0] = x1 + gate_m * (_rms(f) * gains_ref[2:3, :])


def _post_call(x, a, u, mod, gains, pool_scale, w_pool, w_out, w_ff1, w_ff2, tm, ff_chunk):
    b, s, d = x.shape
    d_ff = w_ff1.shape[1]
    nh = tm // HALO
    n_halo = s // HALO
    kern = functools.partial(_post_kernel, tm=tm, seq=s, ff_chunk=ff_chunk)
    const2 = lambda i, j: (0, 0)
    const3 = lambda i, j: (0, 0, 0)
    return pl.pallas_call(
        kern,
        grid=(b, s // tm),
        in_specs=[pl.BlockSpec((1, tm, d), lambda i, j: (i, j, 0)),
                  pl.BlockSpec((1, tm, D_Q), lambda i, j: (i, j, 0)),
                  pl.BlockSpec((1, tm, D_POOL), lambda i, j: (i, j, 0)),
                  pl.BlockSpec((1, HALO, D_POOL), lambda i, j: (i, jnp.maximum(j * nh - 1, 0), 0)),
                  pl.BlockSpec((1, HALO, D_POOL),
                               lambda i, j: (i, jnp.minimum((j + 1) * nh, n_halo - 1), 0)),
                  pl.BlockSpec((1, N_MOD, d), lambda i, j: (i, 0, 0)),
                  pl.BlockSpec((3, d), const2),
                  pl.BlockSpec((1, D_POOL), const2),
                  pl.BlockSpec(w_pool.shape, const3),
                  pl.BlockSpec(w_out.shape, const2, pipeline_mode=pl.Buffered(1)),
                  pl.BlockSpec((d, d_ff), const2, pipeline_mode=pl.Buffered(1)),
                  pl.BlockSpec((d_ff, d), const2, pipeline_mode=pl.Buffered(1))],
        out_specs=pl.BlockSpec((1, tm, d), lambda i, j: (i, j, 0)),
        out_shape=jax.ShapeDtypeStruct((b, s, d), F32),
        scratch_shapes=[pltpu.VMEM((tm + 2 * HALO, D_POOL - k * POOL_GROUP_DIM), F32)
                        for k in range(len(POOL_WINDOWS))],
        compiler_params=pltpu.CompilerParams(dimension_semantics=("arbitrary", "arbitrary"),
                                             vmem_limit_bytes=VMEM_LIMIT),
        name="post_mlp",
    )(x, a, u, u, u, mod, gains, pool_scale, w_pool, w_out, w_ff1, w_ff2)


def _rope_tables(seq_len):
    rows = seq_len // GRID_W
    row_idx = jnp.repeat(jnp.arange(rows, dtype=F32), GRID_W)
    col_idx = jnp.tile(jnp.arange(GRID_W, dtype=F32), rows)
    n_freq = HEAD_DIM // 4
    inv_freq = 1.0 / (ROPE_THETA ** (jnp.arange(n_freq, dtype=F32) / n_freq))
    ang = jnp.concatenate([row_idx[:, None] * inv_freq, col_idx[:, None] * inv_freq], axis=-1)
    cos, sin = jnp.cos(ang), jnp.sin(ang)
    cos_t = jnp.tile(cos, (1, LANES // (HEAD_DIM // 2)))
    sin_t = jnp.tile(jnp.concatenate([-sin, sin], axis=-1), (1, LANES // HEAD_DIM))
    return cos_t, sin_t


def _tiles(s):
    tm_in = min(512, s)
    tm_post = min(512, s)
    tq = min(256, s)
    tk = min(1024, s // 2)
    nq = min(16, s // tq)
    return tm_in, tm_post, tq, tk, nq


def _layer(x, mod, g_pre_mix, g_post_mix, g_pre_mlp, g_post_mlp, w_in, g_q, g_k, w_pool,
           pool_scale, w_out, w_ff1, w_ff2):
    b, s, d = x.shape
    tm_in, tm_post, tq, tk, nq = _tiles(s)
    mod = mod.reshape(b, N_MOD, d)
    cos_t, sin_t = _rope_tables(s)
    gq2 = (jnp.tile(g_q, LANES // HEAD_DIM) * (HEAD_DIM ** -0.5 * LOG2_E)).reshape(1, LANES)
    gk2 = jnp.tile(g_k, LANES // HEAD_DIM).reshape(1, LANES)
    qt, k, vt, u = _in_call(x, mod[:, 0:1], mod[:, 1:2], g_pre_mix.reshape(1, d), w_in, gq2, gk2,
                            cos_t, sin_t, tm_in)
    a = _attn_call(qt, k, vt, tq, tk, nq)
    gains = jnp.stack([g_post_mix, g_pre_mlp, g_post_mlp])
    return _post_call(x, a, u, mod, gains, pool_scale.reshape(1, D_POOL), w_pool, w_out, w_ff1,
                      w_ff2, tm_post, 1024)


def kernel(x_prompt, x_sample, c_prompt, c_sample, w_ada, b_ada, g_pre_mix, g_post_mix,
           g_pre_mlp, g_post_mlp, w_in, g_q, g_k, w_pool, pool_scale, w_out, w_ff1, w_ff2):
    depth = w_ada.shape[0]
    bp, bs = c_prompt.shape[0], c_sample.shape[0]
    pad = (-(bp + bs)) % 8
    y_prompt, y_sample = x_prompt, x_sample
    for l in range(depth):
        c_all = jnp.concatenate([c_prompt, c_sample, jnp.zeros((pad, c_prompt.shape[1]), F32)])
        mod = _mod_call(c_all, w_ada[l], b_ada[l])
        w_in_l = w_in[l]
        w_q = w_in_l[:, :D_Q].reshape(-1, N_KV_HEADS, GROUP, HEAD_DIM).transpose(0, 2, 1, 3)
        w_in_l = jnp.concatenate([w_q.reshape(-1, D_Q), w_in_l[:, D_Q:]], axis=1)
        weights = (g_pre_mix[l], g_post_mix[l], g_pre_mlp[l], g_post_mlp[l],
                   w_in_l.astype(BF16), g_q[l], g_k[l], w_pool[l].astype(BF16), pool_scale[l],
                   w_out[l].astype(BF16), w_ff1[l].astype(BF16), w_ff2[l].astype(BF16))
        y_prompt = _layer(y_prompt, mod[:bp], *weights)
        y_sample = _layer(y_sample, mod[bp:bp + bs], *weights)
    return (y_prompt, y_sample)
```

```python
import functools

import jax
import jax.numpy as jnp
from jax import lax
from jax.experimental import pallas as pl
from jax.experimental.pallas import tpu as pltpu

HEAD_DIM = 64
N_HEADS = 8
N_KV_HEADS = 2
GROUP = N_HEADS // N_KV_HEADS
D_Q = N_HEADS * HEAD_DIM
D_KV = N_KV_HEADS * HEAD_DIM
POOL_WINDOWS = (2, 4, 8, 16)
POOL_GROUP_DIM = 128
D_POOL = POOL_GROUP_DIM * len(POOL_WINDOWS)
GRID_W = 64
ROPE_THETA = 10000.0
EPS = 1e-6
N_MOD = 6
LOG2_E = 1.4426950408889634
LANES = 128
HALO = 16
LVL_ROW0 = HALO - max(POOL_WINDOWS) // 2
assert POOL_WINDOWS == tuple(2 ** (i + 1) for i in range(len(POOL_WINDOWS)))
V_ROWS = HEAD_DIM + 16
VMEM_LIMIT = 56 * 1024 * 1024

F32 = jnp.float32
BF16 = jnp.bfloat16


def _rms(x):
    return x * lax.rsqrt(jnp.mean(x * x, axis=-1, keepdims=True) + EPS)


def _mod_kernel(c_ref, w_ref, b_ref, o_ref):
    c = c_ref[...]
    sc = c / (1.0 + jnp.exp(-c))
    o_ref[...] = jnp.dot(sc, w_ref[...], precision=lax.Precision.HIGHEST,
                         preferred_element_type=F32) + b_ref[...]


def _mod_call(c, w_ada, b_ada):
    m, d = c.shape
    n = w_ada.shape[1]
    tn = 1024
    return pl.pallas_call(
        _mod_kernel,
        grid=(n // tn,),
        in_specs=[pl.BlockSpec((m, d), lambda j: (0, 0)),
                  pl.BlockSpec((d, tn), lambda j: (0, j)),
                  pl.BlockSpec((1, tn), lambda j: (0, j))],
        out_specs=pl.BlockSpec((m, tn), lambda j: (0, j)),
        out_shape=jax.ShapeDtypeStruct((m, n), F32),
        compiler_params=pltpu.CompilerParams(dimension_semantics=("arbitrary",),
                                             vmem_limit_bytes=VMEM_LIMIT),
        name="mod",
    )(c, w_ada, b_ada.reshape(1, n))


def _head_norm_rope(p, gain, cos_t, sin_t, head_ones):
    lane = lax.broadcasted_iota(jnp.int32, p.shape, 1)
    sq = p * p
    sq_hi = sq.astype(BF16)
    sq_lo = (sq - sq_hi.astype(F32)).astype(BF16)
    ms = (jnp.dot(sq_hi, head_ones, preferred_element_type=F32)
          + jnp.dot(sq_lo, head_ones, preferred_element_type=F32)) * (1.0 / HEAD_DIM)
    pn = p * lax.rsqrt(ms + EPS) * gain
    fwd = pltpu.roll(pn, LANES - HEAD_DIM // 2, 1)
    bwd = pltpu.roll(pn, HEAD_DIM // 2, 1)
    rot = jnp.where((lane % HEAD_DIM) < HEAD_DIM // 2, fwd, bwd)
    return pn * cos_t + rot * sin_t


def _in_kernel(x_ref, shift_ref, scale_ref, g_ref, w_ref, gq_ref, gk_ref, cos_ref, sin_ref,
               ones_ref, qt_ref, k_ref, vt_ref, u_ref):
    hm = x_ref.shape[1] // 2
    mult = g_ref[...] * (1.0 + scale_ref[0])

    def half(r):
        rows = slice(r * hm, (r + 1) * hm)
        h = (_rms(x_ref[0, rows, :]) * mult + shift_ref[0]).astype(BF16)
        yield
        proj = jnp.dot(h, w_ref[...], preferred_element_type=F32)
        yield
        cos_t = cos_ref[rows, :]
        sin_t = sin_ref[rows, :]
        for g in range(GROUP):
            qg = _head_norm_rope(proj[:, g * LANES:(g + 1) * LANES], gq_ref[...], cos_t, sin_t,
                                 ones_ref[...])
            qt_ref[0, g, :, rows] = qg.astype(BF16).T
        k = _head_norm_rope(proj[:, D_Q:D_Q + D_KV], gk_ref[...], cos_t, sin_t, ones_ref[...])
        lane = lax.broadcasted_iota(jnp.int32, k.shape, 1)
        vt = proj[:, D_Q + D_KV:D_Q + 2 * D_KV].astype(BF16).T
        for kv in range(N_KV_HEADS):
            own = (lane >= kv * HEAD_DIM) & (lane < (kv + 1) * HEAD_DIM)
            k_ref[0, kv, rows, :] = jnp.where(own, k, 0.0).astype(BF16)
            vt_ref[0, kv, 0:HEAD_DIM, rows] = vt[kv * HEAD_DIM:(kv + 1) * HEAD_DIM]
            vt_ref[0, kv, HEAD_DIM:, rows] = jnp.ones((V_ROWS - HEAD_DIM, hm), BF16)
        u_ref[0, rows, :] = proj[:, D_Q + 2 * D_KV:].astype(BF16)
        yield

    first, second = half(0), half(1)
    for gen in (first, first, second, first, second, second):
        next(gen)


def _in_call(x, shift, scale, g_pre, w_in, gq2, gk2, cos_t, sin_t, tm):
    b, s, d = x.shape
    d_in = w_in.shape[1]
    head_id = jnp.arange(LANES) // HEAD_DIM
    head_ones = (head_id[:, None] == head_id[None, :]).astype(BF16)
    row = lambda i, j: (i, 0, 0)
    const2 = lambda i, j: (0, 0)
    return pl.pallas_call(
        _in_kernel,
        grid=(b, s // tm),
        in_specs=[pl.BlockSpec((1, tm, d), lambda i, j: (i, j, 0)),
                  pl.BlockSpec((1, 1, d), row),
                  pl.BlockSpec((1, 1, d), row),
                  pl.BlockSpec((1, d), const2),
                  pl.BlockSpec((d, d_in), const2),
                  pl.BlockSpec((1, LANES), const2),
                  pl.BlockSpec((1, LANES), const2),
                  pl.BlockSpec((tm, LANES), lambda i, j: (j, 0)),
                  pl.BlockSpec((tm, LANES), lambda i, j: (j, 0)),
                  pl.BlockSpec((LANES, LANES), const2)],
        out_specs=[pl.BlockSpec((1, GROUP, LANES, tm), lambda i, j: (i, 0, 0, j)),
                   pl.BlockSpec((1, N_KV_HEADS, tm, D_KV), lambda i, j: (i, 0, j, 0)),
                   pl.BlockSpec((1, N_KV_HEADS, V_ROWS, tm), lambda i, j: (i, 0, 0, j)),
                   pl.BlockSpec((1, tm, D_POOL), lambda i, j: (i, j, 0))],
        out_shape=[jax.ShapeDtypeStruct((b, GROUP, LANES, s), BF16),
                   jax.ShapeDtypeStruct((b, N_KV_HEADS, s, D_KV), BF16),
                   jax.ShapeDtypeStruct((b, N_KV_HEADS, V_ROWS, s), BF16),
                   jax.ShapeDtypeStruct((b, s, D_POOL), BF16)],
        compiler_params=pltpu.CompilerParams(dimension_semantics=("arbitrary", "arbitrary"),
                                             vmem_limit_bytes=VMEM_LIMIT),
        name="in_proj",
    )(x, shift, scale, g_pre, w_in, gq2, gk2, cos_t, sin_t, head_ones)


def _attn_kernel(qt_ref, k_ref, vt_ref, o_ref, s_scr, p_scr, acc_scr, *, tq, tk, nk, nq):
    mq = GROUP * tq
    n_flat = nq * nk
    nk_bits = nk.bit_length() - 1
    acc_scr[...] = jnp.zeros(acc_scr.shape, F32)

    def scores(f, slot):
        qoff = pl.multiple_of(lax.shift_right_logical(f, nk_bits) * tq, tq)
        koff = pl.multiple_of((f & (nk - 1)) * tk, tk)
        q_t = jnp.concatenate([qt_ref[0, g, :, pl.ds(qoff, tq)] for g in range(GROUP)], axis=1)
        s_t = jnp.dot(k_ref[0, 0, pl.ds(koff, tk), :], q_t, preferred_element_type=F32)
        s_b = s_t.astype(BF16)
        s_scr[slot] = s_b
        return jnp.max(s_b, axis=0, keepdims=True).astype(F32)

    def numerator(f, slot, cmax, m_prev):
        m_prev = jnp.where((f & (nk - 1)) == 0, -jnp.inf, m_prev)
        m_new = jnp.maximum(m_prev, cmax)
        p_scr[slot] = jnp.exp2(s_scr[slot] - m_new.astype(BF16))
        return m_new, jnp.exp2(m_prev - m_new)

    def values(f, slot, alpha):
        koff = pl.multiple_of((f & (nk - 1)) * tk, tk)
        pv = jnp.dot(vt_ref[0, 0, :, pl.ds(koff, tk)], p_scr[slot], preferred_element_type=F32)
        acc_scr[...] = alpha * acc_scr[...] + pv

    def finalize(tile):
        acc = acc_scr[...]
        o_t = acc[0:HEAD_DIM] / acc[HEAD_DIM:HEAD_DIM + 1]
        qoff = pl.multiple_of(tile * tq, tq)
        o_ref[0, pl.ds(qoff, tq), :] = jnp.concatenate(
            [o_t[:, g * tq:(g + 1) * tq].T for g in range(GROUP)], axis=1).astype(BF16)

    m = jnp.full((1, mq), -jnp.inf, F32)
    cmax = scores(0, 0)
    cmax_next = scores(1, 1)
    m, alpha = numerator(0, 0, cmax, m)
    cmax = cmax_next

    def body(t, carry):
        cmax, m, alpha = carry
        f = 2 * t + 2
        for par in range(2):
            cmax_next = scores(f + par, par)
            m, alpha_next = numerator(f + par - 1, 1 - par, cmax, m)
            values(f + par - 2, par, alpha)
            cmax, alpha = cmax_next, alpha_next

        @pl.when((f & (nk - 1)) == 0)
        def _():
            finalize(lax.shift_right_logical(f, nk_bits) - 1)

        return cmax, m, alpha

    cmax, m, alpha = lax.fori_loop(0, (n_flat - 2) // 2, body, (cmax, m, alpha))
    m, alpha_last = numerator(n_flat - 1, 1, cmax, m)
    values(n_flat - 2, 0, alpha)
    values(n_flat - 1, 1, alpha_last)
    finalize(nq - 1)


def _attn_call(qt, k, vt, tq, tk, nq):
    b, _, _, s = qt.shape
    nk = s // tk
    assert nk >= 2 and nk & (nk - 1) == 0 and s % (nq * tq) == 0
    kern = functools.partial(_attn_kernel, tq=tq, tk=tk, nk=nk, nq=nq)
    gw = GROUP * HEAD_DIM
    mq = GROUP * tq
    return pl.pallas_call(
        kern,
        grid=(b, N_KV_HEADS, s // (nq * tq)),
        in_specs=[pl.BlockSpec((1, GROUP, LANES, nq * tq), lambda i, h, j: (i, 0, 0, j)),
                  pl.BlockSpec((1, 1, s, D_KV), lambda i, h, j: (i, h, 0, 0)),
                  pl.BlockSpec((1, 1, V_ROWS, s), lambda i, h, j: (i, h, 0, 0))],
        out_specs=pl.BlockSpec((1, nq * tq, gw), lambda i, h, j: (i, j, h)),
        out_shape=jax.ShapeDtypeStruct((b, s, D_Q), BF16),
        scratch_shapes=[pltpu.VMEM((2, tk, mq), BF16),
                        pltpu.VMEM((2, tk, mq), BF16),
                        pltpu.VMEM((V_ROWS, mq), F32)],
        compiler_params=pltpu.CompilerParams(
            dimension_semantics=("arbitrary", "arbitrary", "arbitrary"),
            vmem_limit_bytes=VMEM_LIMIT),
        name="attention",
    )(qt, k, vt)


def _post_kernel(x_ref, a_ref, u_ref, up_ref, un_ref, mod_ref, gains_ref, ps_ref,
                 wp_ref, wo_ref, w1_ref, w2_ref, y_ref, ext_scr, *lvl_scrs, tm, seq, ff_chunk):
    j = pl.program_id(1)
    nj = pl.num_programs(1)
    u_main = u_ref[0].astype(F32)
    ext_scr[0:HALO, :] = jnp.where(j > 0, up_ref[0].astype(F32), 0.0)
    ext_scr[HALO:HALO + tm, :] = u_main
    ext_scr[HALO + tm:, :] = jnp.where(j < nj - 1, un_ref[0].astype(F32), 0.0)
    t = j * tm + lax.broadcasted_iota(jnp.int32, (tm, 1), 0)
    pooled = []
    src, first = ext_scr, 0
    for gi, w in enumerate(POOL_WINDOWS):
        cols = slice(gi * POOL_GROUP_DIM, (gi + 1) * POOL_GROUP_DIM)
        half = w // 2
        c0 = (gi - first) * POOL_GROUP_DIM
        own = slice(c0, c0 + POOL_GROUP_DIM)
        acc = src[HALO - half:HALO - half + tm, own] + src[HALO:HALO + tm, own]
        if gi + 1 < len(POOL_WINDOWS):
            dst = lvl_scrs[gi]
            rows = tm + 2 * HALO - 4 * (gi + 2) - LVL_ROW0
            dst[LVL_ROW0:LVL_ROW0 + rows, :] = (
                src[LVL_ROW0:LVL_ROW0 + rows, c0 + POOL_GROUP_DIM:]
                + src[LVL_ROW0 + half:LVL_ROW0 + half + rows, c0 + POOL_GROUP_DIM:])
            src, first = dst, gi + 1
        lo = jnp.clip(t - w // 2, 0, seq)
        hi = jnp.clip(t - w // 2 + w, 0, seq)
        cnt = (hi - lo).astype(F32)
        mixed = (acc / cnt - u_main[:, cols]).astype(BF16)
        pg = jnp.dot(mixed, wp_ref[gi], preferred_element_type=F32)
        pooled.append(pg * ps_ref[:, cols])
    p = jnp.concatenate(pooled, axis=1).astype(BF16)

    gate_a = mod_ref[0, 2:3, :]
    shift_m = mod_ref[0, 3:4, :]
    scale_m = mod_ref[0, 4:5, :]
    gate_m = mod_ref[0, 5:6, :]
    d_ff = w1_ref.shape[1]
    n_ff = d_ff // ff_chunk
    hm = tm // 2

    def half(r):
        rows = slice(r * hm, (r + 1) * hm)
        mix = (jnp.dot(a_ref[0, rows, :], wo_ref[0:D_Q, :], preferred_element_type=F32)
               + jnp.dot(p[rows], wo_ref[D_Q:, :], preferred_element_type=F32))
        yield
        x1 = x_ref[0, rows, :] + gate_a * (_rms(mix) * gains_ref[0:1, :])
        h = (_rms(x1) * (gains_ref[1:2, :] * (1.0 + scale_m)) + shift_m).astype(BF16)
        yield
        f = jnp.zeros(x1.shape, F32)
        for c in range(n_ff):
            hc = jnp.dot(h, w1_ref[:, c * ff_chunk:(c + 1) * ff_chunk],
                         preferred_element_type=F32)
            hc = jnp.square(jnp.maximum(hc, 0.0)).astype(BF16)
            f = f + jnp.dot(hc, w2_ref[c * ff_chunk:(c + 1) * ff_chunk, :],
                            preferred_element_type=F32)
            yield
        y_ref[0, rows, :] = x1 + gate_m * (_rms(f) * gains_ref[2:3, :])
        yield

    first, second = half(0), half(1)
    order = [first, first, second, first, second] + [first] * (n_ff - 1) + [second, first]
    order += [second] * n_ff
    for gen in order:
        next(gen)


def _post_call(x, a, u, mod, gains, pool_scale, w_pool, w_out, w_ff1, w_ff2, tm, ff_chunk):
    b, s, d = x.shape
    d_ff = w_ff1.shape[1]
    nh = tm // HALO
    n_halo = s // HALO
    kern = functools.partial(_post_kernel, tm=tm, seq=s, ff_chunk=ff_chunk)
    const2 = lambda i, j: (0, 0)
    const3 = lambda i, j: (0, 0, 0)
    return pl.pallas_call(
        kern,
        grid=(b, s // tm),
        in_specs=[pl.BlockSpec((1, tm, d), lambda i, j: (i, j, 0)),
                  pl.BlockSpec((1, tm, D_Q), lambda i, j: (i, j, 0)),
                  pl.BlockSpec((1, tm, D_POOL), lambda i, j: (i, j, 0)),
                  pl.BlockSpec((1, HALO, D_POOL), lambda i, j: (i, jnp.maximum(j * nh - 1, 0), 0)),
                  pl.BlockSpec((1, HALO, D_POOL),
                               lambda i, j: (i, jnp.minimum((j + 1) * nh, n_halo - 1), 0)),
                  pl.BlockSpec((1, N_MOD, d), lambda i, j: (i, 0, 0)),
                  pl.BlockSpec((3, d), const2),
                  pl.BlockSpec((1, D_POOL), const2),
                  pl.BlockSpec(w_pool.shape, const3),
                  pl.BlockSpec(w_out.shape, const2, pipeline_mode=pl.Buffered(1)),
                  pl.BlockSpec((d, d_ff), const2, pipeline_mode=pl.Buffered(1)),
                  pl.BlockSpec((d_ff, d), const2, pipeline_mode=pl.Buffered(1))],
        out_specs=pl.BlockSpec((1, tm, d), lambda i, j: (i, j, 0)),
        out_shape=jax.ShapeDtypeStruct((b, s, d), F32),
        scratch_shapes=[pltpu.VMEM((tm + 2 * HALO, D_POOL - k * POOL_GROUP_DIM), F32)
                        for k in range(len(POOL_WINDOWS))],
        compiler_params=pltpu.CompilerParams(dimension_semantics=("arbitrary", "arbitrary"),
                                             vmem_limit_bytes=VMEM_LIMIT),
        name="post_mlp",
    )(x, a, u, u, u, mod, gains, pool_scale, w_pool, w_out, w_ff1, w_ff2)


def _rope_tables(seq_len):
    rows = seq_len // GRID_W
    row_idx = jnp.repeat(jnp.arange(rows, dtype=F32), GRID_W)
    col_idx = jnp.tile(jnp.arange(GRID_W, dtype=F32), rows)
    n_freq = HEAD_DIM // 4
    inv_freq = 1.0 / (ROPE_THETA ** (jnp.arange(n_freq, dtype=F32) / n_freq))
    ang = jnp.concatenate([row_idx[:, None] * inv_freq, col_idx[:, None] * inv_freq], axis=-1)
    cos, sin = jnp.cos(ang), jnp.sin(ang)
    cos_t = jnp.tile(cos, (1, LANES // (HEAD_DIM // 2)))
    sin_t = jnp.tile(jnp.concatenate([-sin, sin], axis=-1), (1, LANES // HEAD_DIM))
    return cos_t, sin_t


def _tiles(s):
    tm_in = min(1024, s)
    tm_post = min(512, s)
    tq = min(256, s)
    tk = min(1024, s // 2)
    nq = min(16, s // tq)
    return tm_in, tm_post, tq, tk, nq


def _layer(x, mod, g_pre_mix, g_post_mix, g_pre_mlp, g_post_mlp, w_in, g_q, g_k, w_pool,
           pool_scale, w_out, w_ff1, w_ff2):
    b, s, d = x.shape
    tm_in, tm_post, tq, tk, nq = _tiles(s)
    mod = mod.reshape(b, N_MOD, d)
    cos_t, sin_t = _rope_tables(s)
    gq2 = (jnp.tile(g_q, LANES // HEAD_DIM) * (HEAD_DIM ** -0.5 * LOG2_E)).reshape(1, LANES)
    gk2 = jnp.tile(g_k, LANES // HEAD_DIM).reshape(1, LANES)
    qt, k, vt, u = _in_call(x, mod[:, 0:1], mod[:, 1:2], g_pre_mix.reshape(1, d), w_in, gq2, gk2,
                            cos_t, sin_t, tm_in)
    a = _attn_call(qt, k, vt, tq, tk, nq)
    gains = jnp.stack([g_post_mix, g_pre_mlp, g_post_mlp])
    return _post_call(x, a, u, mod, gains, pool_scale.reshape(1, D_POOL), w_pool, w_out, w_ff1,
                      w_ff2, tm_post, 1024)


def kernel(x_prompt, x_sample, c_prompt, c_sample, w_ada, b_ada, g_pre_mix, g_post_mix,
           g_pre_mlp, g_post_mlp, w_in, g_q, g_k, w_pool, pool_scale, w_out, w_ff1, w_ff2):
    depth = w_ada.shape[0]
    bp, bs = c_prompt.shape[0], c_sample.shape[0]
    pad = (-(bp + bs)) % 8
    y_prompt, y_sample = x_prompt, x_sample
    for l in range(depth):
        c_all = jnp.concatenate([c_prompt, c_sample, jnp.zeros((pad, c_prompt.shape[1]), F32)])
        mod = _mod_call(c_all, w_ada[l], b_ada[l])
        w_in_l = w_in[l]
        w_q = w_in_l[:, :D_Q].reshape(-1, N_KV_HEADS, GROUP, HEAD_DIM).transpose(0, 2, 1, 3)
        w_in_l = jnp.concatenate([w_q.reshape(-1, D_Q), w_in_l[:, D_Q:]], axis=1)
        weights = (g_pre_mix[l], g_post_mix[l], g_pre_mlp[l], g_post_mlp[l],
                   w_in_l.astype(BF16), g_q[l], g_k[l], w_pool[l].astype(BF16), pool_scale[l],
                   w_out[l].astype(BF16), w_ff1[l].astype(BF16), w_ff2[l].astype(BF16))
        y_prompt = _layer(y_prompt, mod[:bp], *weights)
        y_sample = _layer(y_sample, mod[bp:bp + bs], *weights)
    return (y_prompt, y_sample)
```
